```python
import math, functools
import jax, jax.numpy as jnp
from jax import lax
import numpy as np

D_MODEL = 2048
BATCH = 4
SEQ = 2048
DEPTH = 2
DEC_BATCH = 128
DEC_SEQ = 1
PAST_LEN = 8192
PAGE_SIZE = 128

N_HEADS_A = 16
Q_LORA_RANK = 512
KV_LORA_RANK = 512
NOPE_DIM_A = 128
ROPE_DIM_A = 64
V_DIM_A = 128
N_HEADS_B = 8
HEAD_DIM_B = 128
ROT_DIM_B = HEAD_DIM_B // 4
D_FF = 5632
N_EXPERTS = 8
TOP_K = 2
D_EXPERT = 7168
ROPE_THETA = 500000.0
RMS_EPS = 1e-6
Q_BLOCK = 128
N_MOD = 6
SCALE_A = (NOPE_DIM_A + ROPE_DIM_A) ** -0.5
SCALE_B = HEAD_DIM_B ** -0.5
OFF_CKV = Q_LORA_RANK
OFF_KPE = OFF_CKV + KV_LORA_RANK
OFF_QB = OFF_KPE + ROPE_DIM_A
OFF_KB = OFF_QB + N_HEADS_B * 2 * HEAD_DIM_B
OFF_VB = OFF_KB + 2 * HEAD_DIM_B
OFF_GA = OFF_VB + 2 * HEAD_DIM_B
OFF_GB = OFF_GA + D_MODEL
IN_WIDTH = OFF_GB + D_MODEL
IN_SPLITS = (OFF_CKV, OFF_KPE, OFF_QB, OFF_KB, OFF_VB, OFF_GA, OFF_GB)

kernel_name = "hybrid_mla_diffattn_adaln_decoder_step"


def rms_norm(x, g=None):
    xf = x.astype(jnp.float32)
    y = xf * lax.rsqrt(jnp.mean(xf * xf, axis=-1, keepdims=True) + RMS_EPS)
    if g is not None:
        y = y * g.astype(jnp.float32)
    return y.astype(x.dtype)


def rope(x, pos, rot_dim):
    half = rot_dim // 2
    inv_freq = ROPE_THETA ** (-jnp.arange(half, dtype=jnp.float32) / half)
    ang = pos.astype(jnp.float32)[:, None] * inv_freq[None, :]
    cos = jnp.cos(ang)[:, None, :]
    sin = jnp.sin(ang)[:, None, :]
    x1 = x[..., :half].astype(jnp.float32)
    x2 = x[..., half:rot_dim].astype(jnp.float32)
    rot = jnp.concatenate([x1 * cos - x2 * sin, x2 * cos + x1 * sin], axis=-1).astype(x.dtype)
    return jnp.concatenate([rot, x[..., rot_dim:]], axis=-1)


def adaln_params(c, w_mod, b_mod):
    mod = jax.nn.silu(c) @ w_mod + b_mod
    return jnp.split(mod[:, None, :], N_MOD, axis=-1)


def swiglu(h, w_gate, w_up, w_down):
    return (jax.nn.silu(h @ w_gate) * (h @ w_up)) @ w_down


def moe_ffn(h, w_router, w_gate, w_up, w_down):
    logits = (h @ w_router).astype(jnp.float32)
    top_val, top_idx = lax.top_k(logits, TOP_K)
    top_w = jax.nn.softmax(top_val, axis=-1)
    gates = jnp.sum(jax.nn.one_hot(top_idx, N_EXPERTS, dtype=jnp.float32) * top_w[..., None], axis=-2).astype(h.dtype)
    y = jnp.zeros_like(h)
    for e in range(N_EXPERTS):
        y = y + gates[..., e:e + 1] * swiglu(h, w_gate[e], w_up[e], w_down[e])
    return y


def causal_attn(q, k, v, scale):
    b, s_len, m, h, dk = q.shape
    n_blk = s_len // Q_BLOCK
    q_blocks = jnp.moveaxis(q.reshape(b, n_blk, Q_BLOCK, m, h, dk), 1, 0)
    k_pos = jnp.arange(s_len)

    def one_block(args):
        q_i, i = args
        s = jnp.einsum('bqmhd,bkmd->bmhqk', q_i, k).astype(jnp.float32) * scale
        q_pos = i * Q_BLOCK + jnp.arange(Q_BLOCK)
        s = jnp.where(k_pos[None, :] <= q_pos[:, None], s, -jnp.inf)
        p = jax.nn.softmax(s, axis=-1)
        return jnp.einsum('bmhqk,bkd->bqmhd', p, v.astype(jnp.float32)).astype(v.dtype)

    o = lax.map(one_block, (q_blocks, jnp.arange(n_blk)))
    return jnp.moveaxis(o, 0, 1).reshape(b, s_len, m, h, v.shape[-1])


def paged_attn(q, k_new, v_new, gather_k, gather_v, page_table, scale):
    t = q.shape[1]
    s = jnp.einsum('btmhd,bsmd->bmhts', q, k_new).astype(jnp.float32) * scale
    causal = jnp.arange(t)[None, :] <= jnp.arange(t)[:, None]
    s = jnp.where(causal, s, -jnp.inf)
    m0 = jnp.max(s, axis=-1)
    p = jnp.exp(s - m0[..., None])
    l0 = jnp.sum(p, axis=-1)
    acc0 = jnp.einsum('bmhts,bsd->bmhtd', p, v_new.astype(jnp.float32))

    def page_step(carry, pages):
        m, l, acc = carry
        s = jnp.einsum('btmhd,bsmd->bmhts', q, gather_k(pages)).astype(jnp.float32) * scale
        m_new = jnp.maximum(m, jnp.max(s, axis=-1))
        corr = jnp.exp(m - m_new)
        p = jnp.exp(s - m_new[..., None])
        l = l * corr + jnp.sum(p, axis=-1)
        acc = acc * corr[..., None] + jnp.einsum('bmhts,bsd->bmhtd', p, gather_v(pages).astype(jnp.float32))
        return (m_new, l, acc), None

    (m, l, acc), _ = lax.scan(page_step, (m0, l0, acc0), page_table.T)
    out = acc / l[..., None]
    return jnp.transpose(out, (0, 3, 1, 2, 4)).astype(v_new.dtype)


def trunk_layer(x, c, pos, attend, channel_mixer, lam, lam_init,
                w_mod, b_mod, w_in, g_q_norm, w_uq, g_kv_norm, w_uk, w_uv,
                g_subln, w_branch_a, w_branch_b, w_out):
    b, s_len, _ = x.shape
    shift_m, scale_m, gate_m, shift_f, scale_f, gate_f = adaln_params(c, w_mod, b_mod)
    h = rms_norm(x) * (1 + scale_m) + shift_m
    c_q, c_kv, k_pe, q_d, k_d, v_d, g_a, g_b = jnp.split(h @ w_in, IN_SPLITS, axis=-1)
    q = (rms_norm(c_q, g_q_norm) @ w_uq).reshape(b, s_len, N_HEADS_A, NOPE_DIM_A + ROPE_DIM_A)
    q_pe = rope(q[..., NOPE_DIM_A:], pos, ROPE_DIM_A)
    q_lat = jnp.einsum('bshn,chn->bshc', q[..., :NOPE_DIM_A], w_uk)
    q_a = jnp.concatenate([q_lat, q_pe], axis=-1)[:, :, None]
    ckv = rms_norm(c_kv, g_kv_norm)
    kpe = rope(k_pe[:, :, None, :], pos, ROPE_DIM_A)
    k_a = jnp.concatenate([ckv[:, :, None, :], kpe], axis=-1)
    q_b = rope(q_d.reshape(b, s_len, 2 * N_HEADS_B, HEAD_DIM_B), pos, ROT_DIM_B)
    q_b = jnp.swapaxes(q_b.reshape(b, s_len, N_HEADS_B, 2, HEAD_DIM_B), 2, 3)
    k_b = rope(k_d.reshape(b, s_len, 2, HEAD_DIM_B), pos, ROT_DIM_B)
    o_a_lat, o_b = attend(q_a, k_a, ckv, q_b, k_b, v_d)
    o_a = jnp.einsum('bshc,chd->bshd', o_a_lat[:, :, 0], w_uv).reshape(b, s_len, N_HEADS_A * V_DIM_A)
    diff = o_b[:, :, 0] - lam.astype(o_b.dtype) * o_b[:, :, 1]
    o_b = (rms_norm(diff, g_subln) * (1 - lam_init)).reshape(b, s_len, N_HEADS_B * 2 * HEAD_DIM_B)
    mixed = jax.nn.sigmoid(g_a) * (o_a @ w_branch_a) + jax.nn.sigmoid(g_b) * (o_b @ w_branch_b)
    x = x + gate_m * (mixed @ w_out)
    h = rms_norm(x) * (1 + scale_f) + shift_f
    x = x + gate_f * channel_mixer(h)
    return x, (ckv, kpe[:, :, 0], k_b.reshape(b, s_len, 2 * HEAD_DIM_B), v_d)


def setup_inputs(seed: int = 0) -> dict:
    key = jax.random.key(seed)
    ks = iter(jax.random.split(key, 48))
    f32 = jnp.float32

    def nrm(shape, std):
        return std * jax.random.normal(next(ks), shape, f32)

    def gain(shape):
        return 1.0 + 0.02 * jax.random.normal(next(ks), shape, f32)

    n_pages = PAST_LEN // PAGE_SIZE
    n_used = DEC_BATCH * n_pages
    n_pool = n_used + max(1, n_used // 4)
    n_dense = (DEPTH + 1) // 2
    n_moe = DEPTH // 2
    x_prompt = jax.random.normal(next(ks), (BATCH, SEQ, D_MODEL), f32)
    x_sample = jax.random.normal(next(ks), (DEC_BATCH, DEC_SEQ, D_MODEL), f32)
    c_prompt = jax.random.normal(next(ks), (BATCH, D_MODEL), f32)
    c_sample = jax.random.normal(next(ks), (DEC_BATCH, D_MODEL), f32)
    cache_mla_ckv = jax.random.normal(next(ks), (DEPTH, n_pool, PAGE_SIZE, KV_LORA_RANK), f32)
    cache_mla_kpe = jax.random.normal(next(ks), (DEPTH, n_pool, PAGE_SIZE, ROPE_DIM_A), f32)
    cache_diff_k = jax.random.normal(next(ks), (DEPTH, n_pool, PAGE_SIZE, 2 * HEAD_DIM_B), f32)
    cache_diff_v = jax.random.normal(next(ks), (DEPTH, n_pool, PAGE_SIZE, 2 * HEAD_DIM_B), f32)
    page_table = jax.random.permutation(next(ks), n_pool)[:n_used].reshape(DEC_BATCH, n_pages).astype(jnp.int32)
    d_a = N_HEADS_A * V_DIM_A
    d_b = N_HEADS_B * 2 * HEAD_DIM_B
    return {
        'x_prompt': x_prompt,
        'x_sample': x_sample,
        'c_prompt': c_prompt,
        'c_sample': c_sample,
        'cache_mla_ckv': cache_mla_ckv,
        'cache_mla_kpe': cache_mla_kpe,
        'cache_diff_k': cache_diff_k,
        'cache_diff_v': cache_diff_v,
        'page_table': page_table,
        'w_mod': nrm((DEPTH, D_MODEL, N_MOD * D_MODEL), 0.5 * D_MODEL ** -0.5),
        'b_mod': nrm((DEPTH, N_MOD * D_MODEL), 0.02),
        'w_in': nrm((DEPTH, D_MODEL, IN_WIDTH), D_MODEL ** -0.5),
        'g_q_norm': gain((DEPTH, Q_LORA_RANK)),
        'w_uq': nrm((DEPTH, Q_LORA_RANK, N_HEADS_A * (NOPE_DIM_A + ROPE_DIM_A)), Q_LORA_RANK ** -0.5),
        'g_kv_norm': gain((DEPTH, KV_LORA_RANK)),
        'w_uk': nrm((DEPTH, KV_LORA_RANK, N_HEADS_A, NOPE_DIM_A), KV_LORA_RANK ** -0.5),
        'w_uv': nrm((DEPTH, KV_LORA_RANK, N_HEADS_A, V_DIM_A), KV_LORA_RANK ** -0.5),
        'lambda_q1': nrm((DEPTH, HEAD_DIM_B), 0.1),
        'lambda_k1': nrm((DEPTH, HEAD_DIM_B), 0.1),
        'lambda_q2': nrm((DEPTH, HEAD_DIM_B), 0.1),
        'lambda_k2': nrm((DEPTH, HEAD_DIM_B), 0.1),
        'g_subln': gain((DEPTH, 2 * HEAD_DIM_B)),
        'w_branch_a': nrm((DEPTH, d_a, D_MODEL), d_a ** -0.5),
        'w_branch_b': nrm((DEPTH, d_b, D_MODEL), d_b ** -0.5),
        'w_out': nrm((DEPTH, D_MODEL, D_MODEL), D_MODEL ** -0.5),
        'w_ffn_gate': nrm((n_dense, D_MODEL, D_FF), D_MODEL ** -0.5),
        'w_ffn_up': nrm((n_dense, D_MODEL, D_FF), D_MODEL ** -0.5),
        'w_ffn_down': nrm((n_dense, D_FF, D_MODEL), D_FF ** -0.5),
        'w_router': nrm((n_moe, D_MODEL, N_EXPERTS), D_MODEL ** -0.5),
        'w_exp_gate': nrm((n_moe, N_EXPERTS, D_MODEL, D_EXPERT), D_MODEL ** -0.5),
        'w_exp_up': nrm((n_moe, N_EXPERTS, D_MODEL, D_EXPERT), D_MODEL ** -0.5),
        'w_exp_down': nrm((n_moe, N_EXPERTS, D_EXPERT, D_MODEL), D_EXPERT ** -0.5),
        'g_final': gain((D_MODEL,)),
    }


def reference(x_prompt, x_sample, c_prompt, c_sample, cache_mla_ckv, cache_mla_kpe, cache_diff_k, cache_diff_v,
              page_table, w_mod, b_mod, w_in, g_q_norm, w_uq, g_kv_norm, w_uk, w_uv,
              lambda_q1, lambda_k1, lambda_q2, lambda_k2, g_subln, w_branch_a, w_branch_b, w_out,
              w_ffn_gate, w_ffn_up, w_ffn_down, w_router, w_exp_gate, w_exp_up, w_exp_down, g_final):
    f32 = jnp.float32
    past_len = page_table.shape[1] * cache_mla_ckv.shape[2]
    pos_p = jnp.arange(x_prompt.shape[1], dtype=jnp.int32)
    pos_s = past_len + jnp.arange(x_sample.shape[1], dtype=jnp.int32)
    x_p, x_s = x_prompt, x_sample
    rows_p = ([], [], [], [])
    rows_s = ([], [], [], [])
    for l in range(DEPTH):
        lam_init = 0.8 - 0.6 * math.exp(-0.3 * l)
        lam = (jnp.exp(jnp.sum(lambda_q1[l].astype(f32) * lambda_k1[l].astype(f32)))
               - jnp.exp(jnp.sum(lambda_q2[l].astype(f32) * lambda_k2[l].astype(f32))) + lam_init)
        if l % 2 == 0:
            channel_mixer = functools.partial(swiglu, w_gate=w_ffn_gate[l // 2], w_up=w_ffn_up[l // 2],
                                              w_down=w_ffn_down[l // 2])
        else:
            channel_mixer = functools.partial(moe_ffn, w_router=w_router[l // 2], w_gate=w_exp_gate[l // 2],
                                              w_up=w_exp_up[l // 2], w_down=w_exp_down[l // 2])
        layer_w = (w_mod[l], b_mod[l], w_in[l], g_q_norm[l], w_uq[l], g_kv_norm[l], w_uk[l], w_uv[l],
                   g_subln[l], w_branch_a[l], w_branch_b[l], w_out[l])

        def attend_prompt(q_a, k_a, v_a, q_b, k_b, v_b):
            return causal_attn(q_a, k_a, v_a, SCALE_A), causal_attn(q_b, k_b, v_b, SCALE_B)

        def attend_sample(q_a, k_a, v_a, q_b, k_b, v_b, l=l):
            def mla_k(pages):
                return jnp.concatenate([cache_mla_ckv[l, pages], cache_mla_kpe[l, pages]], axis=-1)[:, :, None, :]

            def mla_v(pages):
                return cache_mla_ckv[l, pages]

            def diff_k(pages):
                kp = cache_diff_k[l, pages]
                return kp.reshape(kp.shape[0], kp.shape[1], 2, HEAD_DIM_B)

            def diff_v(pages):
                return cache_diff_v[l, pages]

            o_a = paged_attn(q_a, k_a, v_a, mla_k, mla_v, page_table, SCALE_A)
            o_b = paged_attn(q_b, k_b, v_b, diff_k, diff_v, page_table, SCALE_B)
            return o_a, o_b

        x_p, new_p = trunk_layer(x_p, c_prompt, pos_p, attend_prompt, channel_mixer, lam, lam_init, *layer_w)
        x_s, new_s = trunk_layer(x_s, c_sample, pos_s, attend_sample, channel_mixer, lam, lam_init, *layer_w)
        for buf, r in zip(rows_p, new_p):
            buf.append(r)
        for buf, r in zip(rows_s, new_s):
            buf.append(r)
    y_prompt = rms_norm(x_p, g_final)
    y_sample = rms_norm(x_s, g_final)
    new_mla_ckv_prompt = jnp.stack(rows_p[0])
    new_mla_kpe_prompt = jnp.stack(rows_p[1])
    new_diff_k_prompt = jnp.stack(rows_p[2])
    new_diff_v_prompt = jnp.stack(rows_p[3])
    new_mla_ckv_sample = jnp.stack(rows_s[0])
    new_mla_kpe_sample = jnp.stack(rows_s[1])
    new_diff_k_sample = jnp.stack(rows_s[2])
    new_diff_v_sample = jnp.stack(rows_s[3])
    return (y_prompt, y_sample, new_mla_ckv_prompt, new_mla_kpe_prompt, new_diff_k_prompt, new_diff_v_prompt,
            new_mla_ckv_sample, new_mla_kpe_sample, new_diff_k_sample, new_diff_v_sample)
```

```python
import functools
import math

import jax
import jax.numpy as jnp
from jax import lax
from jax.experimental import pallas as pl
from jax.experimental.pallas import tpu as pltpu

F32 = jnp.float32
BF16 = jnp.bfloat16

D = 2048
NB_PROMPT = 4
SEQ = 2048
DEPTH = 2
DB = 128
PAGE = 128
HA = 16
QL = 512
KVL = 512
NOPE = 128
ROPE_A = 64
VA = 128
HB = 8
HDB = 128
ROT_B = 32
DFF = 5632
NE = 8
DEXP = 7168
THETA = 500000.0
EPS = 1e-6
N_MOD = 6
SCALE_A = (NOPE + ROPE_A) ** -0.5
SCALE_B = HDB ** -0.5
OFF_CKV = QL
OFF_KPE = OFF_CKV + KVL
OFF_QB = OFF_KPE + ROPE_A
OFF_KB = OFF_QB + HB * 2 * HDB
OFF_VB = OFF_KB + 2 * HDB
OFF_GA = OFF_VB + 2 * HDB
OFF_GB = OFF_GA + D

LANES = 128
QHEAD = 2 * LANES

BM = 512
TP = NB_PROMPT * SEQ
TS = DB
TPAD = TP + BM
NBP = TP // BM
NBLK = NBP + 1
BPB = SEQ // BM
BQ = 512
PAGES_PER_STEP = 32
BMO = 256
RPAD = ((2 * (TP + TS) + NE * BMO + BMO - 1) // BMO) * BMO
NT = RPAD // BMO
MIB = 1024 * 1024


def _cp(sem, vmem_mb):
    return pltpu.CompilerParams(dimension_semantics=sem, vmem_limit_bytes=int(vmem_mb * MIB))


def _sigmoid(x):
    return 1.0 / (1.0 + jnp.exp(-x))


def _rope(x, c, s1, s2, half):
    return x * c + pltpu.roll(x, LANES - half, 1) * s1 + pltpu.roll(x, half, 1) * s2


def _sel_mod(i, p_ref, s_ref):
    return jnp.where(i < NBP, p_ref[0], s_ref[...])


def _adaln(c_all, w_mod, b_mod):
    rows = c_all.shape[0]
    bn = 1024
    n = w_mod.shape[2]

    def kern(c_ref, w_ref, b_ref, o_ref):
        c = c_ref[...]
        a = (c * _sigmoid(c)).astype(BF16)
        o_ref[...] = jnp.dot(a, w_ref[...].astype(BF16), preferred_element_type=F32) + b_ref[...]

    return pl.pallas_call(
        kern,
        grid=(DEPTH, n // bn),
        in_specs=[
            pl.BlockSpec((rows, D), lambda l, j: (0, 0)),
            pl.BlockSpec((None, D, bn), lambda l, j: (l, 0, j)),
            pl.BlockSpec((None, 1, bn), lambda l, j: (l, 0, j)),
        ],
        out_specs=pl.BlockSpec((None, rows, bn), lambda l, j: (l, 0, j)),
        out_shape=jax.ShapeDtypeStruct((DEPTH, rows, n), F32),
        compiler_params=_cp(("arbitrary", "arbitrary"), 40),
        name="adaln",
    )(c_all, w_mod, b_mod.reshape(DEPTH, 1, n))


def _mod_specs():
    return [
        pl.BlockSpec((1, 1, D), lambda i: (jnp.minimum(i // BPB, NB_PROMPT - 1), 0, 0)),
        pl.BlockSpec((BM, D), lambda i: (0, 0)),
    ]


def _ln_mod(x, scale, shift, w_router_pad=None):
    route = w_router_pad is not None

    def kern(x_ref, scp, scs, shp, shs, *rest):
        i = pl.program_id(0)
        xv = x_ref[...]
        xn = xv * lax.rsqrt(jnp.mean(xv * xv, axis=-1, keepdims=True) + EPS)
        h = xn * (1.0 + _sel_mod(i, scp, scs)) + _sel_mod(i, shp, shs)
        if not route:
            (h_ref,) = rest
            h_ref[...] = h.astype(BF16)
            return
        rw_ref, h_ref, idx_ref, wt_ref = rest
        h_ref[...] = h
        logits = jnp.dot(h, rw_ref[...], preferred_element_type=F32, precision=lax.Precision.HIGHEST)
        lane = lax.broadcasted_iota(jnp.int32, logits.shape, 1).astype(F32)
        neg = jnp.float32(-jnp.inf)
        lg = jnp.where(lane < NE, logits, neg)
        m1 = jnp.max(lg, axis=-1, keepdims=True)
        i1 = jnp.min(jnp.where(lg == m1, lane, float(LANES)), axis=-1, keepdims=True)
        lg2 = jnp.where(lane == i1, neg, lg)
        m2 = jnp.max(lg2, axis=-1, keepdims=True)
        i2 = jnp.min(jnp.where(lg2 == m2, lane, float(LANES)), axis=-1, keepdims=True)
        e = jnp.exp(m2 - m1)
        den = 1.0 + e
        idx_ref[...] = jnp.where(lane == 0, i1, jnp.where(lane == 1, i2, 0.0)).astype(jnp.int32)
        wt_ref[...] = jnp.where(lane == 0, 1.0 / den, jnp.where(lane == 1, e / den, 0.0))

    in_specs = [pl.BlockSpec((BM, D), lambda i: (i, 0))] + _mod_specs() + _mod_specs()
    args = [x, scale[0], scale[1], shift[0], shift[1]]
    out_specs = [pl.BlockSpec((BM, D), lambda i: (i, 0))]
    out_shape = [jax.ShapeDtypeStruct((TPAD, D), F32 if route else BF16)]
    if route:
        in_specs.append(pl.BlockSpec((D, LANES), lambda i: (0, 0)))
        args.append(w_router_pad)
        out_specs += [pl.BlockSpec((BM, LANES), lambda i: (i, 0))] * 2
        out_shape += [jax.ShapeDtypeStruct((TPAD, LANES), jnp.int32), jax.ShapeDtypeStruct((TPAD, LANES), F32)]
    res = pl.pallas_call(
        kern,
        grid=(NBLK,),
        in_specs=in_specs,
        out_specs=out_specs,
        out_shape=out_shape,
        compiler_params=_cp(("arbitrary",), 40),
        name="ln_mod_route" if route else "ln_mod",
    )(*args)
    return res if route else res[0]


def _final_norm(x, g, nblk, blk0, rows):
    def kern(x_ref, g_ref, o_ref):
        xv = x_ref[...]
        o_ref[...] = xv * lax.rsqrt(jnp.mean(xv * xv, axis=-1, keepdims=True) + EPS) * g_ref[...]

    return pl.pallas_call(
        kern,
        grid=(nblk,),
        in_specs=[pl.BlockSpec((BM, D), lambda i: (i + blk0, 0)), pl.BlockSpec((1, D), lambda i: (0, 0))],
        out_specs=pl.BlockSpec((BM, D), lambda i: (i, 0)),
        out_shape=jax.ShapeDtypeStruct((rows, D), F32),
        compiler_params=_cp(("arbitrary",), 32),
        name="final_norm",
    )(x, g.reshape(1, D))


def _mm(name, xs, ws, pairs, extras, outs, epilogue, *, bn, nrb=NBLK, vmem_mb=48):
    n = ws[0].shape[1]
    x_ops = []
    x_slots = []
    for x in xs:
        parts = x if isinstance(x, tuple) else (x,)
        x_slots.append((len(x_ops), len(parts)))
        x_ops.extend(parts)
    nx, nw, nex, no = len(x_ops), len(ws), len(extras), len(outs)

    def kern(*refs):
        x_refs = refs[:nx]
        w_refs = refs[nx:nx + nw]
        e_refs = refs[nx + nw:nx + nw + nex]
        o_refs = refs[nx + nw + nex:nx + nw + nex + no]
        wb_refs = refs[nx + nw + nex + no:]
        j = pl.program_id(0)
        i = pl.program_id(1)

        @pl.when(i == 0)
        def _():
            for w_ref, wb_ref in zip(w_refs, wb_refs):
                wb_ref[...] = w_ref[...].astype(BF16)

        def x_val(a):
            lo, cnt = x_slots[a]
            if cnt == 1:
                return x_refs[lo][...]
            return jnp.where(i < NBP, x_refs[lo][...], x_refs[lo + 1][...])

        accs = [jnp.dot(x_val(a), wb_refs[b][...], preferred_element_type=F32) for a, b in pairs]
        epilogue(accs, e_refs, o_refs, i, j)

    in_specs = []
    for x in xs:
        if isinstance(x, tuple):
            in_specs.append(pl.BlockSpec((BM, x[0].shape[1]), lambda j, i: (jnp.minimum(i, NBP - 1), 0)))
            in_specs.append(pl.BlockSpec((BM, x[1].shape[1]), lambda j, i: (0, 0)))
        else:
            in_specs.append(pl.BlockSpec((BM, x.shape[1]), lambda j, i: (i, 0)))
    w_mode = dict(pipeline_mode=pl.Buffered(1)) if n == bn else {}
    in_specs += [pl.BlockSpec((w.shape[0], bn), lambda j, i: (0, j), **w_mode) for w in ws]
    in_specs += [pl.BlockSpec(bs, im) for _, bs, im in extras]
    res = pl.pallas_call(
        kern,
        grid=(n // bn, nrb),
        in_specs=in_specs,
        out_specs=[pl.BlockSpec(bs, im) for _, bs, im in outs],
        out_shape=[sd for sd, _, _ in outs],
        scratch_shapes=[pltpu.VMEM((w.shape[0], bn), BF16) for w in ws],
        compiler_params=_cp(("arbitrary", "arbitrary"), vmem_mb),
        name=name,
    )(*x_ops, *ws, *[a for a, _, _ in extras])
    return res


def _row_extra(arr, width=None):
    width = arr.shape[1] if width is None else width
    return (arr, (BM, width), lambda j, i: (i, 0))


def _const_extra(arr):
    return (arr, arr.shape, lambda j, i: (0,) * arr.ndim)


def _tile_out(rows, cols, dtype, bn):
    return (jax.ShapeDtypeStruct((rows, cols), dtype), (BM, bn), lambda j, i: (i, j))


def _mod_extras(mod, bn):
    return [
        (mod[0], (1, 1, bn), lambda j, i: (jnp.minimum(i // BPB, NB_PROMPT - 1), 0, j)),
        (mod[1], (BM, bn), lambda j, i: (0, j)),
    ]


def _proj_qd(h, w_qd, tabs_b):
    bn = 1024

    def epi(accs, e, o, i, j):
        c, s1, s2 = e[0][...], e[1][...], e[2][...]
        acc = accs[0]
        for k in range(bn // LANES):
            sl = slice(k * LANES, (k + 1) * LANES)
            o[0][:, sl] = _rope(acc[:, sl], c, s1, s2, ROT_B // 2).astype(BF16)

    return _mm("proj_qd", [h], [w_qd], [(0, 0)], [_row_extra(t) for t in tabs_b],
               [_tile_out(TPAD, HB * 2 * HDB, BF16, bn)], epi, bn=bn, vmem_mb=40)[0]


def _proj_gates(h, w_g):
    bn = 1024

    def epi(accs, e, o, i, j):
        o[0][...] = accs[0]

    return _mm("proj_gates", [h], [w_g], [(0, 0)], [], [_tile_out(TPAD, 2 * D, F32, bn)], epi,
               bn=bn, vmem_mb=40)[0]


W_SMALL = QL + KVL + 2 * HDB + 2 * HDB + LANES


def _proj_small(h, w_small, g_q, g_kv, tabs_a, tabs_b):
    def rms(x, g):
        return x * lax.rsqrt(jnp.mean(x * x, axis=-1, keepdims=True) + EPS) * g

    def epi(accs, e, o, i, j):
        acc = accs[0]
        gq, gkv = e[0][...], e[1][...]
        ca, s1a, s2a = e[2][...], e[3][...], e[4][...]
        cb, s1b, s2b = e[5][...], e[6][...], e[7][...]
        o[0][...] = rms(acc[:, 0:QL], gq).astype(BF16)
        ckv = rms(acc[:, QL:QL + KVL], gkv)
        o[1][...] = ckv
        o[2][...] = ckv.astype(BF16)
        off = QL + KVL
        for k in range(2):
            sl = slice(k * LANES, (k + 1) * LANES)
            kd = _rope(acc[:, off + k * LANES:off + (k + 1) * LANES], cb, s1b, s2b, ROT_B // 2)
            o[3][:, sl] = kd
            o[4][:, sl] = kd.astype(BF16)
        off += 2 * HDB
        vd = acc[:, off:off + 2 * HDB]
        o[5][...] = vd
        o[6][...] = vd.astype(BF16)
        off += 2 * HDB
        kpe = _rope(acc[:, off:off + LANES], ca, s1a, s2a, ROPE_A // 2)
        o[7][...] = kpe
        o[8][...] = kpe.astype(BF16)

    def full(cols, dtype):
        return (jax.ShapeDtypeStruct((TPAD, cols), dtype), (BM, cols), lambda j, i: (i, 0))

    extras = [_const_extra(g_q.reshape(1, QL)), _const_extra(g_kv.reshape(1, KVL))]
    extras += [_row_extra(t) for t in tabs_a] + [_row_extra(t) for t in tabs_b]
    outs = [full(QL, BF16), full(KVL, F32), full(KVL, BF16), full(2 * HDB, F32), full(2 * HDB, BF16),
            full(2 * HDB, F32), full(2 * HDB, BF16), full(LANES, F32), full(LANES, BF16)]
    return _mm("proj_small", [h], [w_small], [(0, 0)], extras, outs, epi, bn=W_SMALL, vmem_mb=52)


def _q_up(cqn, w_uq_pad, tabs_a):
    bn = 1024

    def epi(accs, e, o, i, j):
        c, s1, s2 = e[0][...], e[1][...], e[2][...]
        acc = accs[0]
        for k in range(bn // QHEAD):
            lo = k * QHEAD
            o[0][:, lo:lo + LANES] = acc[:, lo:lo + LANES].astype(BF16)
            o[0][:, lo + LANES:lo + QHEAD] = _rope(acc[:, lo + LANES:lo + QHEAD], c, s1, s2,
                                                   ROPE_A // 2).astype(BF16)

    return _mm("q_up", [cqn], [w_uq_pad], [(0, 0)], [_row_extra(t) for t in tabs_a],
               [_tile_out(TPAD, HA * QHEAD, BF16, bn)], epi, bn=bn, vmem_mb=32)[0]


def _kv_up(ckv_bf, kpe_bf, w_uk_flat, w_uv_flat):
    bn = 512

    def epi(accs, e, o, i, j):
        kn, v = accs
        kpe = e[0][...]
        for k in range(bn // NOPE):
            o[0][:, k * QHEAD:k * QHEAD + LANES] = kn[:, k * NOPE:(k + 1) * NOPE].astype(BF16)
            o[0][:, k * QHEAD + LANES:(k + 1) * QHEAD] = kpe
        o[1][...] = v.astype(BF16)

    outs = [
        (jax.ShapeDtypeStruct((TPAD, HA * QHEAD), BF16), (BM, 2 * bn), lambda j, i: (i, j)),
        _tile_out(TPAD, HA * VA, BF16, bn),
    ]
    return _mm("kv_up", [ckv_bf], [w_uk_flat, w_uv_flat], [(0, 0), (0, 1)], [_row_extra(kpe_bf)], outs, epi,
               bn=bn, vmem_mb=32)


def _merge(o_a, o_b, w_ba, w_bb, gates):
    bn = 512
    ncb = D // bn

    def epi(accs, e, o, i, j):
        o[0][...] = (_sigmoid(e[0][...]) * accs[0] + _sigmoid(e[1][...]) * accs[1]).astype(BF16)

    extras = [(gates, (BM, bn), lambda j, i: (i, j)), (gates, (BM, bn), lambda j, i: (i, j + ncb))]
    return _mm("merge", [o_a, o_b], [w_ba, w_bb], [(0, 0), (1, 1)], extras, [_tile_out(TPAD, D, BF16, bn)], epi,
               bn=bn, vmem_mb=48)[0]


def _residual_mm(name, a, w, x_old, gate, bn, vmem_mb):
    def epi(accs, e, o, i, j):
        o[0][...] = e[0][...] + _sel_mod(i, e[1], e[2]) * accs[0]

    extras = [(x_old, (BM, bn), lambda j, i: (i, j))] + _mod_extras(gate, bn)
    return _mm(name, [a], [w], [(0, 0)], extras, [_tile_out(TPAD, D, F32, bn)], epi, bn=bn, vmem_mb=vmem_mb)[0]


def _ffn_gate_up(h, w_gate, w_up):
    bn = 512

    def epi(accs, e, o, i, j):
        g, u = accs
        o[0][...] = (g * _sigmoid(g) * u).astype(BF16)

    return _mm("ffn_gate_up", [h], [w_gate, w_up], [(0, 0), (0, 1)], [], [_tile_out(TPAD, DFF, BF16, bn)], epi,
               bn=bn, vmem_mb=40)[0]


def _causal_head(q_ref, k_ref, v_ref, qi, qcols, kcols, scale, mask):
    rows = slice(qi * BQ, (qi + 1) * BQ)
    qb = q_ref[rows, qcols]
    pieces = []
    for kj in range(qi + 1):
        kb = k_ref[kj * BQ:(kj + 1) * BQ, kcols]
        s = lax.dot_general(qb, kb, (((1,), (1,)), ((), ())), preferred_element_type=F32) * scale
        if kj == qi:
            s = jnp.where(mask, s, -jnp.inf)
        pieces.append(s)
    m = functools.reduce(jnp.maximum, [jnp.max(s, axis=-1, keepdims=True) for s in pieces])
    l = None
    acc = None
    for kj, s in enumerate(pieces):
        p = jnp.exp(s - m)
        ps = jnp.sum(p, axis=-1, keepdims=True)
        pv = jnp.dot(p.astype(BF16), v_ref[kj * BQ:(kj + 1) * BQ, :], preferred_element_type=F32)
        l = ps if l is None else l + ps
        acc = pv if acc is None else acc + pv
    return acc / l


def _causal_mask():
    r = lax.broadcasted_iota(jnp.int32, (BQ, BQ), 0)
    c = lax.broadcasted_iota(jnp.int32, (BQ, BQ), 1)
    return c <= r


def _mla_prompt_attn(q, k, v):
    def kern(q_ref, k_ref, v_ref, o_ref):
        mask = _causal_mask()
        for qi in range(SEQ // BQ):
            o = _causal_head(q_ref, k_ref, v_ref, qi, slice(None), slice(None), SCALE_A, mask)
            o_ref[qi * BQ:(qi + 1) * BQ, :] = o.astype(BF16)

    return pl.pallas_call(
        kern,
        grid=(NB_PROMPT, HA),
        in_specs=[
            pl.BlockSpec((SEQ, QHEAD), lambda b, h: (b, h)),
            pl.BlockSpec((SEQ, QHEAD), lambda b, h: (b, h)),
            pl.BlockSpec((SEQ, VA), lambda b, h: (b, h)),
        ],
        out_specs=pl.BlockSpec((SEQ, VA), lambda b, h: (b, h)),
        out_shape=jax.ShapeDtypeStruct((TP, HA * VA), BF16),
        compiler_params=_cp(("arbitrary", "arbitrary"), 48),
        name="mla_prompt_attn",
    )(q, k, v)


def _lambda_value(lq1, lk1, lq2, lk2, lam_init):
    a = jnp.sum(lq1[...] * lk1[...], axis=-1, keepdims=True)
    b = jnp.sum(lq2[...] * lk2[...], axis=-1, keepdims=True)
    return jnp.exp(a) - jnp.exp(b) + lam_init


def _subln(diff, g, lam_init):
    y = diff * lax.rsqrt(jnp.mean(diff * diff, axis=-1, keepdims=True) + EPS)
    return y * g * (1.0 - lam_init)


def _diff_prompt_attn(qd, kd, vd, lams, g_subln, lam_init):
    def kern(q_ref, k_ref, v_ref, lq1, lk1, lq2, lk2, g_ref, o_ref):
        mask = _causal_mask()
        lam = _lambda_value(lq1, lk1, lq2, lk2, lam_init)
        for qi in range(SEQ // BQ):
            o0 = _causal_head(q_ref, k_ref, v_ref, qi, slice(0, HDB), slice(0, HDB), SCALE_B, mask)
            o1 = _causal_head(q_ref, k_ref, v_ref, qi, slice(HDB, 2 * HDB), slice(HDB, 2 * HDB), SCALE_B, mask)
            o_ref[qi * BQ:(qi + 1) * BQ, :] = _subln(o0 - lam * o1, g_ref[...], lam_init).astype(BF16)

    vec = pl.BlockSpec((1, HDB), lambda b, h: (0, 0))
    return pl.pallas_call(
        kern,
        grid=(NB_PROMPT, HB),
        in_specs=[
            pl.BlockSpec((SEQ, 2 * HDB), lambda b, h: (b, h)),
            pl.BlockSpec((SEQ, 2 * HDB), lambda b, h: (b, 0)),
            pl.BlockSpec((SEQ, 2 * HDB), lambda b, h: (b, 0)),
            vec, vec, vec, vec,
            pl.BlockSpec((1, 2 * HDB), lambda b, h: (0, 0)),
        ],
        out_specs=pl.BlockSpec((SEQ, 2 * HDB), lambda b, h: (b, h)),
        out_shape=jax.ShapeDtypeStruct((TP, HB * 2 * HDB), BF16),
        compiler_params=_cp(("arbitrary", "arbitrary"), 48),
        name="diff_prompt_attn",
    )(qd, kd, vd, *[x.reshape(1, HDB) for x in lams], g_subln.reshape(1, 2 * HDB))


def _page_spec(layer, shape, k, npages):
    return pl.BlockSpec((None, None) + shape,
                        lambda b, s, pt: (layer, pt[b * npages + s * PAGES_PER_STEP + k], 0, 0))


def _mla_decode(q_lat, q_pe, ckv_new, kpe_new, cache_ckv, cache_kpe_t, page_flat, layer, npages):
    nsteps = npages // PAGES_PER_STEP
    pp = PAGES_PER_STEP
    rows = pp * PAGE

    def kern(pt_ref, ql_ref, qp_ref, cn_ref, pn_ref, *rest):
        ckv_refs = rest[:pp]
        kpe_refs = rest[pp:2 * pp]
        o_ref = rest[2 * pp]
        kbuf, pbuf, m_ref, l_ref, acc_ref = rest[2 * pp + 1:]
        s = pl.program_id(1)

        @pl.when(s == 0)
        def _():
            cn = cn_ref[...]
            s0 = (jnp.sum(ql_ref[...].astype(F32) * cn, axis=-1, keepdims=True)
                  + jnp.sum(qp_ref[...].astype(F32) * pn_ref[...], axis=-1, keepdims=True)) * SCALE_A
            m_ref[...] = s0
            l_ref[...] = jnp.ones(l_ref.shape, F32)
            acc_ref[...] = jnp.broadcast_to(cn, acc_ref.shape)

        for k in range(pp):
            kbuf[k * PAGE:(k + 1) * PAGE, :] = ckv_refs[k][...].astype(BF16)
            pbuf[:, k * PAGE:(k + 1) * PAGE] = kpe_refs[k][...].astype(BF16)
        sc = lax.dot_general(ql_ref[...], kbuf[...], (((1,), (1,)), ((), ())), preferred_element_type=F32)
        sc = (sc + jnp.dot(qp_ref[...], pbuf[...], preferred_element_type=F32)) * SCALE_A
        m_old = m_ref[...]
        m_new = jnp.maximum(m_old, jnp.max(sc, axis=-1, keepdims=True))
        corr = jnp.exp(m_old - m_new)
        p = jnp.exp(sc - m_new)
        l_ref[...] = l_ref[...] * corr + jnp.sum(p, axis=-1, keepdims=True)
        acc_ref[...] = acc_ref[...] * corr + jnp.dot(p.astype(BF16), kbuf[...], preferred_element_type=F32)
        m_ref[...] = m_new

        @pl.when(s == nsteps - 1)
        def _():
            o_ref[...] = (acc_ref[...] / l_ref[...]).astype(BF16)

    in_specs = [
        pl.BlockSpec((None, HA, KVL), lambda b, s, pt: (b, 0, 0)),
        pl.BlockSpec((None, HA, ROPE_A), lambda b, s, pt: (b, 0, 0)),
        pl.BlockSpec((None, 1, KVL), lambda b, s, pt: (b, 0, 0)),
        pl.BlockSpec((None, 1, ROPE_A), lambda b, s, pt: (b, 0, 0)),
    ]
    in_specs += [_page_spec(layer, (PAGE, KVL), k, npages) for k in range(pp)]
    in_specs += [_page_spec(layer, (ROPE_A, PAGE), k, npages) for k in range(pp)]
    grid_spec = pltpu.PrefetchScalarGridSpec(
        num_scalar_prefetch=1,
        grid=(DB, nsteps),
        in_specs=in_specs,
        out_specs=pl.BlockSpec((None, HA, KVL), lambda b, s, pt: (b, 0, 0)),
        scratch_shapes=[pltpu.VMEM((rows, KVL), BF16), pltpu.VMEM((ROPE_A, rows), BF16),
                        pltpu.VMEM((HA, 1), F32), pltpu.VMEM((HA, 1), F32), pltpu.VMEM((HA, KVL), F32)],
    )
    return pl.pallas_call(
        kern,
        grid_spec=grid_spec,
        out_shape=jax.ShapeDtypeStruct((DB, HA, KVL), BF16),
        compiler_params=_cp(("arbitrary", "arbitrary"), 52),
        name="mla_decode",
    )(page_flat, q_lat, q_pe, ckv_new, kpe_new, *([cache_ckv] * pp), *([cache_kpe_t] * pp))


def _diff_decode(q_s, knew, vnew, cache_k, cache_v, page_flat, layer, npages, lams, g_subln, lam_init):
    nsteps = npages // PAGES_PER_STEP
    pp = PAGES_PER_STEP
    rows = pp * PAGE
    nrow = 2 * HB

    def kern(pt_ref, q_ref, kn_ref, vn_ref, lq1, lk1, lq2, lk2, g_ref, *rest):
        k_refs = rest[:pp]
        v_refs = rest[pp:2 * pp]
        o_ref = rest[2 * pp]
        kbuf, vbuf, m_ref, l_ref, acc_ref = rest[2 * pp + 1:]
        s = pl.program_id(1)
        q = q_ref[...]

        row = lax.broadcasted_iota(jnp.int32, (nrow, 1), 0)

        def scores(kmat_or_row, vpu):
            outs = []
            for mp in range(2):
                km = kmat_or_row[:, mp * HDB:(mp + 1) * HDB]
                if vpu:
                    outs.append(jnp.sum(q.astype(F32) * km, axis=-1, keepdims=True))
                else:
                    outs.append(lax.dot_general(q, km, (((1,), (1,)), ((), ())), preferred_element_type=F32))
            return jnp.where(row < HB, outs[0], outs[1]) * SCALE_B

        @pl.when(s == 0)
        def _():
            m_ref[...] = scores(kn_ref[...], True)
            l_ref[...] = jnp.ones(l_ref.shape, F32)
            acc_ref[...] = jnp.broadcast_to(vn_ref[...], acc_ref.shape)

        for k in range(pp):
            kbuf[k * PAGE:(k + 1) * PAGE, :] = k_refs[k][...].astype(BF16)
            vbuf[k * PAGE:(k + 1) * PAGE, :] = v_refs[k][...].astype(BF16)
        sc = scores(kbuf[...], False)
        m_old = m_ref[...]
        m_new = jnp.maximum(m_old, jnp.max(sc, axis=-1, keepdims=True))
        corr = jnp.exp(m_old - m_new)
        p = jnp.exp(sc - m_new)
        l_ref[...] = l_ref[...] * corr + jnp.sum(p, axis=-1, keepdims=True)
        acc_ref[...] = acc_ref[...] * corr + jnp.dot(p.astype(BF16), vbuf[...], preferred_element_type=F32)
        m_ref[...] = m_new

        @pl.when(s == nsteps - 1)
        def _():
            o = acc_ref[...] / l_ref[...]
            lam = _lambda_value(lq1, lk1, lq2, lk2, lam_init)
            o_ref[...] = _subln(o[0:HB, :] - lam * o[HB:nrow, :], g_ref[...], lam_init)

    vec = pl.BlockSpec((1, HDB), lambda b, s, pt: (0, 0))
    in_specs = [
        pl.BlockSpec((None, nrow, HDB), lambda b, s, pt: (b, 0, 0)),
        pl.BlockSpec((None, 1, 2 * HDB), lambda b, s, pt: (b, 0, 0)),
        pl.BlockSpec((None, 1, 2 * HDB), lambda b, s, pt: (b, 0, 0)),
        vec, vec, vec, vec,
        pl.BlockSpec((1, 2 * HDB), lambda b, s, pt: (0, 0)),
    ]
    in_specs += [_page_spec(layer, (PAGE, 2 * HDB), k, npages) for k in range(pp)]
    in_specs += [_page_spec(layer, (PAGE, 2 * HDB), k, npages) for k in range(pp)]
    grid_spec = pltpu.PrefetchScalarGridSpec(
        num_scalar_prefetch=1,
        grid=(DB, nsteps),
        in_specs=in_specs,
        out_specs=pl.BlockSpec((None, HB, 2 * HDB), lambda b, s, pt: (b, 0, 0)),
        scratch_shapes=[pltpu.VMEM((rows, 2 * HDB), BF16), pltpu.VMEM((rows, 2 * HDB), BF16),
                        pltpu.VMEM((nrow, 1), F32), pltpu.VMEM((nrow, 1), F32), pltpu.VMEM((nrow, 2 * HDB), F32)],
    )
    return pl.pallas_call(
        kern,
        grid_spec=grid_spec,
        out_shape=jax.ShapeDtypeStruct((DB, HB, 2 * HDB), F32),
        compiler_params=_cp(("arbitrary", "arbitrary"), 48),
        name="diff_decode",
    )(page_flat, q_s, knew, vnew, *[x.reshape(1, HDB) for x in lams], g_subln.reshape(1, 2 * HDB),
      *([cache_k] * pp), *([cache_v] * pp))


def _q_absorb(q, w_uk_flat):
    rb = TP // TS

    def kern(q_ref, w_ref, o_ref):
        o_ref[...] = lax.dot_general(q_ref[...], w_ref[...].astype(BF16), (((1,), (1,)), ((), ())),
                                     preferred_element_type=F32).astype(BF16)

    return pl.pallas_call(
        kern,
        grid=(HA,),
        in_specs=[pl.BlockSpec((TS, NOPE), lambda h: (rb, 2 * h)), pl.BlockSpec((KVL, NOPE), lambda h: (0, h))],
        out_specs=pl.BlockSpec((None, TS, KVL), lambda h: (h, 0, 0)),
        out_shape=jax.ShapeDtypeStruct((HA, TS, KVL), BF16),
        compiler_params=_cp(("arbitrary",), 16),
        name="q_absorb",
    )(q, w_uk_flat)


def _v_up_sample(o_lat_t, w_uv_flat):
    def kern(o_ref, w_ref, out_ref):
        out_ref[0:TS, :] = jnp.dot(o_ref[...], w_ref[...].astype(BF16), preferred_element_type=F32).astype(BF16)
        out_ref[TS:BM, :] = jnp.zeros((BM - TS, VA), BF16)

    return pl.pallas_call(
        kern,
        grid=(HA,),
        in_specs=[pl.BlockSpec((None, TS, KVL), lambda h: (h, 0, 0)), pl.BlockSpec((KVL, VA), lambda h: (0, h))],
        out_specs=pl.BlockSpec((BM, VA), lambda h: (0, h)),
        out_shape=jax.ShapeDtypeStruct((BM, HA * VA), BF16),
        compiler_params=_cp(("arbitrary",), 16),
        name="v_up_sample",
    )(o_lat_t, w_uv_flat)


def _route_meta(idx, wts):
    row = jnp.arange(TPAD, dtype=jnp.int32)
    valid = row < TP + TS
    e = jnp.where(valid[:, None], idx, NE).reshape(-1)
    onehot = (e[:, None] == jnp.arange(NE, dtype=jnp.int32)[None, :]).astype(jnp.int32)
    csum = jnp.cumsum(onehot, axis=0)
    rank = jnp.sum((csum - onehot) * onehot, axis=1)
    counts = csum[-1]
    padded = ((counts + BMO - 1) // BMO) * BMO
    gend = jnp.cumsum(padded)
    gstart = gend - padded
    is_real = e < NE
    pos = jnp.where(is_real, gstart[jnp.minimum(e, NE - 1)] + rank, RPAD).astype(jnp.int32)
    token = jnp.repeat(row, 2)
    row_token = jnp.zeros((RPAD,), jnp.int32).at[pos].set(token, mode="drop")
    row_w = jnp.zeros((RPAD,), F32).at[pos].set(wts.reshape(-1), mode="drop")
    tile_start = jnp.arange(NT, dtype=jnp.int32) * BMO
    total = gend[-1]
    n_valid = (total // BMO).astype(jnp.int32)
    te = jnp.searchsorted(gend, tile_start, side="right").astype(jnp.int32)
    te_last = te[jnp.maximum(n_valid - 1, 0)]
    tile_valid = tile_start < total
    te = jnp.where(tile_valid, jnp.minimum(te, NE - 1), te_last)
    pos_tok = jnp.where(is_real, pos, 0)
    return row_token, row_w, te, n_valid.reshape(1), pos_tok


def _moe_gather(h32, row_token, n_valid):
    def kern(tok_ref, nv_ref, h_ref, o_ref, buf, sem):
        r = pl.program_id(0)

        @pl.when(r < nv_ref[0])
        def _():
            def start(k, c):
                t = tok_ref[r * BMO + k]
                pltpu.make_async_copy(h_ref.at[pl.ds(t, 1)], buf.at[pl.ds(k, 1)], sem).start()
                return c

            lax.fori_loop(0, BMO, start, 0)

            def wait(k, c):
                pltpu.make_async_copy(h_ref.at[pl.ds(0, 1)], buf.at[pl.ds(k, 1)], sem).wait()
                return c

            lax.fori_loop(0, BMO, wait, 0)
            o_ref[...] = buf[...].astype(BF16)

        @pl.when(r >= nv_ref[0])
        def _():
            o_ref[...] = jnp.zeros(o_ref.shape, BF16)

    grid_spec = pltpu.PrefetchScalarGridSpec(
        num_scalar_prefetch=2,
        grid=(NT,),
        in_specs=[pl.BlockSpec(memory_space=pl.ANY)],
        out_specs=pl.BlockSpec((BMO, D), lambda r, tok, nv: (r, 0)),
        scratch_shapes=[pltpu.VMEM((BMO, D), F32), pltpu.SemaphoreType.DMA(())],
    )
    return pl.pallas_call(
        kern,
        grid_spec=grid_spec,
        out_shape=jax.ShapeDtypeStruct((RPAD, D), BF16),
        compiler_params=_cp(("arbitrary",), 16),
        name="moe_gather",
    )(row_token, n_valid, h32)


def _moe_gate_up(xs, w_gate, w_up, te, n_valid):
    bf = 1024

    def kern(te_ref, nv_ref, x_ref, wg_ref, wu_ref, o_ref, wgb, wub):
        r = pl.program_id(1)
        prev = te_ref[jnp.maximum(r - 1, 0)]

        @pl.when((r == 0) | (te_ref[r] != prev))
        def _():
            wgb[...] = wg_ref[...].astype(BF16)
            wub[...] = wu_ref[...].astype(BF16)

        @pl.when(r < nv_ref[0])
        def _():
            x = x_ref[...]
            g = jnp.dot(x, wgb[...], preferred_element_type=F32)
            u = jnp.dot(x, wub[...], preferred_element_type=F32)
            o_ref[...] = (g * _sigmoid(g) * u).astype(BF16)

        @pl.when(r >= nv_ref[0])
        def _():
            o_ref[...] = jnp.zeros(o_ref.shape, BF16)

    grid_spec = pltpu.PrefetchScalarGridSpec(
        num_scalar_prefetch=2,
        grid=(DEXP // bf, NT),
        in_specs=[
            pl.BlockSpec((BMO, D), lambda f, r, te, nv: (jnp.minimum(r, nv[0] - 1), 0)),
            pl.BlockSpec((None, D, bf), lambda f, r, te, nv: (te[r], 0, f)),
            pl.BlockSpec((None, D, bf), lambda f, r, te, nv: (te[r], 0, f)),
        ],
        out_specs=pl.BlockSpec((BMO, bf), lambda f, r, te, nv: (r, f)),
        scratch_shapes=[pltpu.VMEM((D, bf), BF16), pltpu.VMEM((D, bf), BF16)],
    )
    return pl.pallas_call(
        kern,
        grid_spec=grid_spec,
        out_shape=jax.ShapeDtypeStruct((RPAD, DEXP), BF16),
        compiler_params=_cp(("arbitrary", "arbitrary"), 52),
        name="moe_gate_up",
    )(te, n_valid, xs, w_gate, w_up)


def _moe_down(a, w_down, row_w, te, n_valid):
    bn = 512

    def kern(te_ref, nv_ref, a_ref, w_ref, rw_ref, o_ref, wb):
        r = pl.program_id(1)
        prev = te_ref[jnp.maximum(r - 1, 0)]

        @pl.when((r == 0) | (te_ref[r] != prev))
        def _():
            wb[...] = w_ref[...].astype(BF16)

        @pl.when(r < nv_ref[0])
        def _():
            o_ref[...] = jnp.dot(a_ref[...], wb[...], preferred_element_type=F32) * rw_ref[...]

        @pl.when(r >= nv_ref[0])
        def _():
            o_ref[...] = jnp.zeros(o_ref.shape, F32)

    grid_spec = pltpu.PrefetchScalarGridSpec(
        num_scalar_prefetch=2,
        grid=(D // bn, NT),
        in_specs=[
            pl.BlockSpec((BMO, DEXP), lambda n, r, te, nv: (jnp.minimum(r, nv[0] - 1), 0)),
            pl.BlockSpec((None, DEXP, bn), lambda n, r, te, nv: (te[r], 0, n)),
            pl.BlockSpec((BMO, 1), lambda n, r, te, nv: (r, 0)),
        ],
        out_specs=pl.BlockSpec((BMO, bn), lambda n, r, te, nv: (r, n)),
        scratch_shapes=[pltpu.VMEM((DEXP, bn), BF16)],
    )
    return pl.pallas_call(
        kern,
        grid_spec=grid_spec,
        out_shape=jax.ShapeDtypeStruct((RPAD, D), F32),
        compiler_params=_cp(("arbitrary", "arbitrary"), 56),
        name="moe_down",
    )(te, n_valid, a, w_down, row_w.reshape(RPAD, 1))


def _moe_combine(x_old, y, pos_tok, gate):
    def kern(pos_ref, x_ref, gp, gs, y_ref, o_ref, buf0, buf1, sem):
        i = pl.program_id(0)

        def start(k, c):
            t = i * BM + k
            pltpu.make_async_copy(y_ref.at[pl.ds(pos_ref[2 * t], 1)], buf0.at[pl.ds(k, 1)], sem).start()
            pltpu.make_async_copy(y_ref.at[pl.ds(pos_ref[2 * t + 1], 1)], buf1.at[pl.ds(k, 1)], sem).start()
            return c

        lax.fori_loop(0, BM, start, 0)

        def wait(k, c):
            pltpu.make_async_copy(y_ref.at[pl.ds(0, 1)], buf0.at[pl.ds(k, 1)], sem).wait()
            pltpu.make_async_copy(y_ref.at[pl.ds(0, 1)], buf1.at[pl.ds(k, 1)], sem).wait()
            return c

        lax.fori_loop(0, BM, wait, 0)
        o_ref[...] = x_ref[...] + _sel_mod(i, gp, gs) * (buf0[...] + buf1[...])

    grid_spec = pltpu.PrefetchScalarGridSpec(
        num_scalar_prefetch=1,
        grid=(NBLK,),
        in_specs=[
            pl.BlockSpec((BM, D), lambda i, pos: (i, 0)),
            pl.BlockSpec((1, 1, D), lambda i, pos: (jnp.minimum(i // BPB, NB_PROMPT - 1), 0, 0)),
            pl.BlockSpec((BM, D), lambda i, pos: (0, 0)),
            pl.BlockSpec(memory_space=pl.ANY),
        ],
        out_specs=pl.BlockSpec((BM, D), lambda i, pos: (i, 0)),
        scratch_shapes=[pltpu.VMEM((BM, D), F32), pltpu.VMEM((BM, D), F32), pltpu.SemaphoreType.DMA(())],
    )
    return pl.pallas_call(
        kern,
        grid_spec=grid_spec,
        out_shape=jax.ShapeDtypeStruct((TPAD, D), F32),
        compiler_params=_cp(("arbitrary",), 48),
        name="moe_combine",
    )(pos_tok, x_old, gate[0], gate[1], y)


def _rope_tables(pos, rot_dim, period):
    half = rot_dim // 2
    lane = jnp.arange(LANES)
    within = lane % period
    active = within < rot_dim
    first = (within % rot_dim) < half
    inv_freq = THETA ** (-jnp.arange(half, dtype=F32) / half)
    ang = pos.astype(F32)[:, None] * inv_freq[None, :]
    cos = jnp.cos(ang)[:, within % half]
    sin = jnp.sin(ang)[:, within % half]
    c = jnp.where(active[None, :], cos, 1.0)
    s1 = jnp.where((active & first)[None, :], -sin, 0.0)
    s2 = jnp.where((active & ~first)[None, :], sin, 0.0)
    return c, s1, s2


def _split_mod(m):
    return m[:NB_PROMPT].reshape(NB_PROMPT, 1, D), jnp.pad(m[NB_PROMPT:NB_PROMPT + TS], ((0, BM - TS), (0, 0)))


def kernel(x_prompt, x_sample, c_prompt, c_sample, cache_mla_ckv, cache_mla_kpe, cache_diff_k, cache_diff_v, page_table, w_mod, b_mod, w_in, g_q_norm, w_uq, g_kv_norm, w_uk, w_uv, lambda_q1, lambda_k1, lambda_q2, lambda_k2, g_subln, w_branch_a, w_branch_b, w_out, w_ffn_gate, w_ffn_up, w_ffn_down, w_router, w_exp_gate, w_exp_up, w_exp_down, g_final):
    npages = page_table.shape[1]
    past_len = npages * cache_mla_ckv.shape[2]
    page_flat = page_table.reshape(-1).astype(jnp.int32)
    cache_kpe_t = jnp.swapaxes(cache_mla_kpe, 2, 3)

    x = jnp.concatenate([x_prompt.reshape(TP, D), x_sample.reshape(TS, D), jnp.zeros((BM - TS, D), F32)], axis=0)
    c_rows = NB_PROMPT + TS
    c_pad = (-c_rows) % 8
    c_all = jnp.concatenate([c_prompt, c_sample, jnp.zeros((c_pad, D), F32)], axis=0)
    mods = _adaln(c_all, w_mod, b_mod)

    pos = jnp.concatenate([jnp.tile(jnp.arange(SEQ, dtype=jnp.int32), NB_PROMPT),
                           jnp.full((TS,), past_len, jnp.int32), jnp.zeros((BM - TS,), jnp.int32)])
    tabs_a = _rope_tables(pos, ROPE_A, ROPE_A)
    tabs_b = _rope_tables(pos, ROT_B, LANES)

    rows_p = ([], [], [], [])
    rows_s = ([], [], [], [])
    for l in range(DEPTH):
        lam_init = 0.8 - 0.6 * math.exp(-0.3 * l)
        lams = (lambda_q1[l], lambda_k1[l], lambda_q2[l], lambda_k2[l])
        m6 = [_split_mod(mods[l, :, k * D:(k + 1) * D]) for k in range(N_MOD)]
        shift_m, scale_m, gate_m, shift_f, scale_f, gate_f = m6

        wl = w_in[l]
        w_qd = wl[:, OFF_QB:OFF_KB]
        w_g = wl[:, OFF_GA:]
        w_small = jnp.concatenate([wl[:, :OFF_KPE], wl[:, OFF_KB:OFF_GA], wl[:, OFF_KPE:OFF_QB],
                                   jnp.zeros((D, LANES - ROPE_A), F32)], axis=1)
        wq = w_uq[l].reshape(QL, HA, NOPE + ROPE_A)
        w_uq_pad = jnp.concatenate([wq, jnp.zeros((QL, HA, QHEAD - NOPE - ROPE_A), F32)], axis=2).reshape(QL, HA * QHEAD)
        w_uk_flat = w_uk[l].reshape(KVL, HA * NOPE)
        w_uv_flat = w_uv[l].reshape(KVL, HA * VA)

        h = _ln_mod(x, scale_m, shift_m)
        qd = _proj_qd(h, w_qd, tabs_b)
        gates = _proj_gates(h, w_g)
        cqn, ckv32, ckvbf, kd32, kdbf, vd32, vdbf, kpe32, kpebf = _proj_small(
            h, w_small, g_q_norm[l], g_kv_norm[l], tabs_a, tabs_b)
        q = _q_up(cqn, w_uq_pad, tabs_a)
        kfull, v = _kv_up(ckvbf, kpebf, w_uk_flat, w_uv_flat)

        o_a = _mla_prompt_attn(q, kfull, v)
        o_b = _diff_prompt_attn(qd, kdbf, vdbf, lams, g_subln[l], lam_init)

        q_lat = jnp.transpose(_q_absorb(q, w_uk_flat), (1, 0, 2))
        q_s = q[TP:TP + TS].reshape(TS, HA, QHEAD)
        o_lat = _mla_decode(q_lat, q_s[:, :, NOPE:NOPE + ROPE_A], ckv32[TP:TP + TS].reshape(TS, 1, KVL),
                            kpe32[TP:TP + TS, :ROPE_A].reshape(TS, 1, ROPE_A), cache_mla_ckv, cache_kpe_t,
                            page_flat, l, npages)
        o_a_s = _v_up_sample(jnp.transpose(o_lat, (1, 0, 2)), w_uv_flat)

        qd_s = qd[TP:TP + TS].reshape(TS, HB, 2, HDB).transpose(0, 2, 1, 3).reshape(TS, 2 * HB, HDB)
        o_b_s = _diff_decode(qd_s, kd32[TP:TP + TS].reshape(TS, 1, 2 * HDB), vd32[TP:TP + TS].reshape(TS, 1, 2 * HDB),
                             cache_diff_k, cache_diff_v, page_flat, l, npages, lams, g_subln[l], lam_init)
        o_b_s = jnp.pad(o_b_s.astype(BF16).reshape(TS, HB * 2 * HDB), ((0, BM - TS), (0, 0)))

        mixed = _merge((o_a, o_a_s), (o_b, o_b_s), w_branch_a[l], w_branch_b[l], gates)
        x = _residual_mm("attn_out", mixed, w_out[l], x, gate_m, 1024, 40)

        if l % 2 == 0:
            h2 = _ln_mod(x, scale_f, shift_f)
            a = _ffn_gate_up(h2, w_ffn_gate[l // 2], w_ffn_up[l // 2])
            x = _residual_mm("ffn_down", a, w_ffn_down[l // 2], x, gate_f, 512, 52)
        else:
            w_router_pad = jnp.pad(w_router[l // 2], ((0, 0), (0, LANES - NE)))
            h2, idx, wts = _ln_mod(x, scale_f, shift_f, w_router_pad)
            row_token, row_w, te, n_valid, pos_tok = _route_meta(idx[:, :2], wts[:, :2])
            xs = _moe_gather(h2, row_token, n_valid)
            a = _moe_gate_up(xs, w_exp_gate[l // 2], w_exp_up[l // 2], te, n_valid)
            y = _moe_down(a, w_exp_down[l // 2], row_w, te, n_valid)
            x = _moe_combine(x, y, pos_tok, gate_f)

        for buf, r in zip(rows_p, (ckv32, kpe32[:, :ROPE_A], kd32, vd32)):
            buf.append(r[:TP].reshape(NB_PROMPT, SEQ, r.shape[1]))
        for buf, r in zip(rows_s, (ckv32, kpe32[:, :ROPE_A], kd32, vd32)):
            buf.append(r[TP:TP + TS].reshape(TS, 1, r.shape[1]))

    y_prompt = _final_norm(x, g_final, NBP, 0, TP).reshape(NB_PROMPT, SEQ, D)
    y_sample = _final_norm(x, g_final, 1, NBP, BM)[:TS].reshape(TS, 1, D)
    return (y_prompt, y_sample,
            jnp.stack(rows_p[0]), jnp.stack(rows_p[1]), jnp.stack(rows_p[2]), jnp.stack(rows_p[3]),
            jnp.stack(rows_s[0]), jnp.stack(rows_s[1]), jnp.stack(rows_s[2]), jnp.stack(rows_s[3]))
```

```python
import functools
import math

import jax
import jax.numpy as jnp
from jax import lax
from jax.experimental import pallas as pl
from jax.experimental.pallas import tpu as pltpu

F32 = jnp.float32
BF16 = jnp.bfloat16

D = 2048
NB_PROMPT = 4
SEQ = 2048
DEPTH = 2
DB = 128
PAGE = 128
HA = 16
QL = 512
KVL = 512
NOPE = 128
ROPE_A = 64
VA = 128
HB = 8
HDB = 128
ROT_B = 32
DFF = 5632
NE = 8
DEXP = 7168
THETA = 500000.0
EPS = 1e-6
N_MOD = 6
SCALE_A = (NOPE + ROPE_A) ** -0.5
SCALE_B = HDB ** -0.5
OFF_CKV = QL
OFF_KPE = OFF_CKV + KVL
OFF_QB = OFF_KPE + ROPE_A
OFF_KB = OFF_QB + HB * 2 * HDB
OFF_VB = OFF_KB + 2 * HDB
OFF_GA = OFF_VB + 2 * HDB
OFF_GB = OFF_GA + D

LANES = 128
QHEAD = 2 * LANES

BM = 512
TP = NB_PROMPT * SEQ
TS = DB
TPAD = TP + BM
NBP = TP // BM
NBLK = NBP + 1
BPB = SEQ // BM
BQ = 512
PAGES_PER_STEP = 64
BMO = 256
RPAD = ((2 * (TP + TS) + NE * BMO + BMO - 1) // BMO) * BMO
NT = RPAD // BMO
MIB = 1024 * 1024


def _cp(sem, vmem_mb):
    return pltpu.CompilerParams(dimension_semantics=sem, vmem_limit_bytes=int(vmem_mb * MIB))


def _sigmoid(x):
    return 1.0 / (1.0 + jnp.exp(-x))


def _rope(x, c, s1, s2, half):
    return x * c + pltpu.roll(x, LANES - half, 1) * s1 + pltpu.roll(x, half, 1) * s2


def _sel_mod(i, p_ref, s_ref):
    return jnp.where(i < NBP, p_ref[0], s_ref[...])


def _adaln(c_all, w_mod, b_mod):
    rows = c_all.shape[0]
    bn = 1024
    n = w_mod.shape[2]

    def kern(c_ref, w_ref, b_ref, o_ref):
        c = c_ref[...]
        a = (c * _sigmoid(c)).astype(BF16)
        o_ref[...] = jnp.dot(a, w_ref[...].astype(BF16), preferred_element_type=F32) + b_ref[...]

    return pl.pallas_call(
        kern,
        grid=(DEPTH, n // bn),
        in_specs=[
            pl.BlockSpec((rows, D), lambda l, j: (0, 0)),
            pl.BlockSpec((None, D, bn), lambda l, j: (l, 0, j)),
            pl.BlockSpec((None, 1, bn), lambda l, j: (l, 0, j)),
        ],
        out_specs=pl.BlockSpec((None, rows, bn), lambda l, j: (l, 0, j)),
        out_shape=jax.ShapeDtypeStruct((DEPTH, rows, n), F32),
        compiler_params=_cp(("arbitrary", "arbitrary"), 40),
        name="adaln",
    )(c_all, w_mod, b_mod.reshape(DEPTH, 1, n))


def _mod_specs():
    return [
        pl.BlockSpec((1, 1, D), lambda i: (jnp.minimum(i // BPB, NB_PROMPT - 1), 0, 0)),
        pl.BlockSpec((BM, D), lambda i: (0, 0)),
    ]


def _ln_mod(x, scale, shift, w_router_pad=None):
    route = w_router_pad is not None

    def kern(x_ref, scp, scs, shp, shs, *rest):
        i = pl.program_id(0)
        xv = x_ref[...]
        xn = xv * lax.rsqrt(jnp.mean(xv * xv, axis=-1, keepdims=True) + EPS)
        h = xn * (1.0 + _sel_mod(i, scp, scs)) + _sel_mod(i, shp, shs)
        if not route:
            (h_ref,) = rest
            h_ref[...] = h.astype(BF16)
            return
        rw_ref, h_ref, idx_ref, wt_ref = rest
        h_ref[...] = h
        logits = jnp.dot(h, rw_ref[...], preferred_element_type=F32, precision=lax.Precision.HIGHEST)
        lane = lax.broadcasted_iota(jnp.int32, logits.shape, 1).astype(F32)
        neg = jnp.float32(-jnp.inf)
        lg = jnp.where(lane < NE, logits, neg)
        m1 = jnp.max(lg, axis=-1, keepdims=True)
        i1 = jnp.min(jnp.where(lg == m1, lane, float(LANES)), axis=-1, keepdims=True)
        lg2 = jnp.where(lane == i1, neg, lg)
        m2 = jnp.max(lg2, axis=-1, keepdims=True)
        i2 = jnp.min(jnp.where(lg2 == m2, lane, float(LANES)), axis=-1, keepdims=True)
        e = jnp.exp(m2 - m1)
        den = 1.0 + e
        idx_ref[...] = jnp.where(lane == 0, i1, jnp.where(lane == 1, i2, 0.0)).astype(jnp.int32)
        wt_ref[...] = jnp.where(lane == 0, 1.0 / den, jnp.where(lane == 1, e / den, 0.0))

    in_specs = [pl.BlockSpec((BM, D), lambda i: (i, 0))] + _mod_specs() + _mod_specs()
    args = [x, scale[0], scale[1], shift[0], shift[1]]
    out_specs = [pl.BlockSpec((BM, D), lambda i: (i, 0))]
    out_shape = [jax.ShapeDtypeStruct((TPAD, D), F32 if route else BF16)]
    if route:
        in_specs.append(pl.BlockSpec((D, LANES), lambda i: (0, 0)))
        args.append(w_router_pad)
        out_specs += [pl.BlockSpec((BM, LANES), lambda i: (i, 0))] * 2
        out_shape += [jax.ShapeDtypeStruct((TPAD, LANES), jnp.int32), jax.ShapeDtypeStruct((TPAD, LANES), F32)]
    res = pl.pallas_call(
        kern,
        grid=(NBLK,),
        in_specs=in_specs,
        out_specs=out_specs,
        out_shape=out_shape,
        compiler_params=_cp(("arbitrary",), 40),
        name="ln_mod_route" if route else "ln_mod",
    )(*args)
    return res if route else res[0]


def _final_norm(x, g, nblk, blk0, rows):
    def kern(x_ref, g_ref, o_ref):
        xv = x_ref[...]
        o_ref[...] = xv * lax.rsqrt(jnp.mean(xv * xv, axis=-1, keepdims=True) + EPS) * g_ref[...]

    return pl.pallas_call(
        kern,
        grid=(nblk,),
        in_specs=[pl.BlockSpec((BM, D), lambda i: (i + blk0, 0)), pl.BlockSpec((1, D), lambda i: (0, 0))],
        out_specs=pl.BlockSpec((BM, D), lambda i: (i, 0)),
        out_shape=jax.ShapeDtypeStruct((rows, D), F32),
        compiler_params=_cp(("arbitrary",), 32),
        name="final_norm",
    )(x, g.reshape(1, D))


def _mm(name, xs, ws, pairs, extras, outs, epilogue, *, bn, nrb=NBLK, vmem_mb=48):
    w_arrs = [w[0] if isinstance(w, tuple) else w for w in ws]
    w_layer = [w[1] if isinstance(w, tuple) else None for w in ws]
    w_kn = [a.shape[-2:] for a in w_arrs]
    n = w_kn[0][1]
    x_ops = []
    x_slots = []
    for x in xs:
        parts = x if isinstance(x, tuple) else (x,)
        x_slots.append((len(x_ops), len(parts)))
        x_ops.extend(parts)
    nx, nw, nex, no = len(x_ops), len(ws), len(extras), len(outs)

    def kern(*refs):
        x_refs = refs[:nx]
        w_refs = refs[nx:nx + nw]
        e_refs = refs[nx + nw:nx + nw + nex]
        o_refs = refs[nx + nw + nex:nx + nw + nex + no]
        wb_refs = refs[nx + nw + nex + no:]
        j = pl.program_id(0)
        i = pl.program_id(1)

        @pl.when(i == 0)
        def _():
            for w_ref, wb_ref in zip(w_refs, wb_refs):
                wb_ref[...] = w_ref[...].astype(BF16)

        def x_val(a):
            lo, cnt = x_slots[a]
            if cnt == 1:
                return x_refs[lo][...]
            return jnp.where(i < NBP, x_refs[lo][...], x_refs[lo + 1][...])

        accs = [jnp.dot(x_val(a), wb_refs[b][...], preferred_element_type=F32) for a, b in pairs]
        epilogue(accs, e_refs, o_refs, i, j)

    in_specs = []
    for x in xs:
        if isinstance(x, tuple):
            in_specs.append(pl.BlockSpec((BM, x[0].shape[1]), lambda j, i: (jnp.minimum(i, NBP - 1), 0)))
            in_specs.append(pl.BlockSpec((BM, x[1].shape[1]), lambda j, i: (0, 0)))
        else:
            in_specs.append(pl.BlockSpec((BM, x.shape[1]), lambda j, i: (i, 0)))
    w_mode = dict(pipeline_mode=pl.Buffered(1)) if n == bn else {}
    for (k_dim, _), layer in zip(w_kn, w_layer):
        if layer is None:
            in_specs.append(pl.BlockSpec((k_dim, bn), lambda j, i: (0, j), **w_mode))
        else:
            in_specs.append(pl.BlockSpec((None, k_dim, bn), lambda j, i, layer=layer: (layer, 0, j), **w_mode))
    in_specs += [pl.BlockSpec(bs, im) for _, bs, im in extras]
    res = pl.pallas_call(
        kern,
        grid=(n // bn, nrb),
        in_specs=in_specs,
        out_specs=[pl.BlockSpec(bs, im) for _, bs, im in outs],
        out_shape=[sd for sd, _, _ in outs],
        scratch_shapes=[pltpu.VMEM((k_dim, bn), BF16) for k_dim, _ in w_kn],
        compiler_params=_cp(("arbitrary", "arbitrary"), vmem_mb),
        name=name,
    )(*x_ops, *w_arrs, *[a for a, _, _ in extras])
    return res


def _row_extra(arr, width=None):
    width = arr.shape[1] if width is None else width
    return (arr, (BM, width), lambda j, i: (i, 0))


def _const_extra(arr):
    return (arr, arr.shape, lambda j, i: (0,) * arr.ndim)


def _tile_out(rows, cols, dtype, bn):
    return (jax.ShapeDtypeStruct((rows, cols), dtype), (BM, bn), lambda j, i: (i, j))


def _mod_extras(mod, bn):
    return [
        (mod[0], (1, 1, bn), lambda j, i: (jnp.minimum(i // BPB, NB_PROMPT - 1), 0, j)),
        (mod[1], (BM, bn), lambda j, i: (0, j)),
    ]


def _proj_qd(h, w_qd, tabs_b):
    bn = 1024

    def epi(accs, e, o, i, j):
        c, s1, s2 = e[0][...], e[1][...], e[2][...]
        acc = accs[0]
        for k in range(bn // LANES):
            sl = slice(k * LANES, (k + 1) * LANES)
            o[0][:, sl] = _rope(acc[:, sl], c, s1, s2, ROT_B // 2).astype(BF16)

    return _mm("proj_qd", [h], [w_qd], [(0, 0)], [_row_extra(t) for t in tabs_b],
               [_tile_out(TPAD, HB * 2 * HDB, BF16, bn)], epi, bn=bn, vmem_mb=40)[0]


def _proj_gates(h, w_g):
    bn = 1024

    def epi(accs, e, o, i, j):
        o[0][...] = accs[0]

    return _mm("proj_gates", [h], [w_g], [(0, 0)], [], [_tile_out(TPAD, 2 * D, F32, bn)], epi,
               bn=bn, vmem_mb=40)[0]


W_SMALL = QL + KVL + 2 * HDB + 2 * HDB + LANES


def _proj_small(h, w_small, g_q, g_kv, tabs_a, tabs_b):
    def rms(x, g):
        return x * lax.rsqrt(jnp.mean(x * x, axis=-1, keepdims=True) + EPS) * g

    def epi(accs, e, o, i, j):
        acc = accs[0]
        gq, gkv = e[0][...], e[1][...]
        ca, s1a, s2a = e[2][...], e[3][...], e[4][...]
        cb, s1b, s2b = e[5][...], e[6][...], e[7][...]
        o[0][...] = rms(acc[:, 0:QL], gq).astype(BF16)
        ckv = rms(acc[:, QL:QL + KVL], gkv)
        o[1][...] = ckv
        o[2][...] = ckv.astype(BF16)
        off = QL + KVL
        for k in range(2):
            sl = slice(k * LANES, (k + 1) * LANES)
            kd = _rope(acc[:, off + k * LANES:off + (k + 1) * LANES], cb, s1b, s2b, ROT_B // 2)
            o[3][:, sl] = kd
            o[4][:, sl] = kd.astype(BF16)
        off += 2 * HDB
        vd = acc[:, off:off + 2 * HDB]
        o[5][...] = vd
        o[6][...] = vd.astype(BF16)
        off += 2 * HDB
        kpe = _rope(acc[:, off:off + LANES], ca, s1a, s2a, ROPE_A // 2)
        o[7][...] = kpe
        o[8][...] = kpe.astype(BF16)

    def full(cols, dtype):
        return (jax.ShapeDtypeStruct((TPAD, cols), dtype), (BM, cols), lambda j, i: (i, 0))

    extras = [_const_extra(g_q.reshape(1, QL)), _const_extra(g_kv.reshape(1, KVL))]
    extras += [_row_extra(t) for t in tabs_a] + [_row_extra(t) for t in tabs_b]
    outs = [full(QL, BF16), full(KVL, F32), full(KVL, BF16), full(2 * HDB, F32), full(2 * HDB, BF16),
            full(2 * HDB, F32), full(2 * HDB, BF16), full(LANES, F32), full(LANES, BF16)]
    return _mm("proj_small", [h], [w_small], [(0, 0)], extras, outs, epi, bn=W_SMALL, vmem_mb=52)


def _q_up(cqn, w_uq_pad, tabs_a):
    bn = 1024

    def epi(accs, e, o, i, j):
        c, s1, s2 = e[0][...], e[1][...], e[2][...]
        acc = accs[0]
        for k in range(bn // QHEAD):
            lo = k * QHEAD
            o[0][:, lo:lo + LANES] = acc[:, lo:lo + LANES].astype(BF16)
            o[0][:, lo + LANES:lo + QHEAD] = _rope(acc[:, lo + LANES:lo + QHEAD], c, s1, s2,
                                                   ROPE_A // 2).astype(BF16)

    return _mm("q_up", [cqn], [w_uq_pad], [(0, 0)], [_row_extra(t) for t in tabs_a],
               [_tile_out(TPAD, HA * QHEAD, BF16, bn)], epi, bn=bn, vmem_mb=32)[0]


def _kv_up(ckv_bf, kpe_bf, w_uk_flat, w_uv_flat):
    bn = 512

    def epi(accs, e, o, i, j):
        kn, v = accs
        kpe = e[0][...]
        for k in range(bn // NOPE):
            o[0][:, k * QHEAD:k * QHEAD + LANES] = kn[:, k * NOPE:(k + 1) * NOPE].astype(BF16)
            o[0][:, k * QHEAD + LANES:(k + 1) * QHEAD] = kpe
        o[1][...] = v.astype(BF16)

    outs = [
        (jax.ShapeDtypeStruct((TPAD, HA * QHEAD), BF16), (BM, 2 * bn), lambda j, i: (i, j)),
        _tile_out(TPAD, HA * VA, BF16, bn),
    ]
    return _mm("kv_up", [ckv_bf], [w_uk_flat, w_uv_flat], [(0, 0), (0, 1)], [_row_extra(kpe_bf)], outs, epi,
               bn=bn, vmem_mb=32)


def _merge(o_a, o_b, w_ba, w_bb, gates):
    bn = 512
    ncb = D // bn

    def epi(accs, e, o, i, j):
        o[0][...] = (_sigmoid(e[0][...]) * accs[0] + _sigmoid(e[1][...]) * accs[1]).astype(BF16)

    extras = [(gates, (BM, bn), lambda j, i: (i, j)), (gates, (BM, bn), lambda j, i: (i, j + ncb))]
    return _mm("merge", [o_a, o_b], [w_ba, w_bb], [(0, 0), (1, 1)], extras, [_tile_out(TPAD, D, BF16, bn)], epi,
               bn=bn, vmem_mb=48)[0]


def _residual_mm(name, a, w, x_old, gate, bn, vmem_mb):
    def epi(accs, e, o, i, j):
        o[0][...] = e[0][...] + _sel_mod(i, e[1], e[2]) * accs[0]

    extras = [(x_old, (BM, bn), lambda j, i: (i, j))] + _mod_extras(gate, bn)
    return _mm(name, [a], [w], [(0, 0)], extras, [_tile_out(TPAD, D, F32, bn)], epi, bn=bn, vmem_mb=vmem_mb)[0]


def _ffn_gate_up(h, w_gate, w_up):
    bn = 512

    def epi(accs, e, o, i, j):
        g, u = accs
        o[0][...] = (g * _sigmoid(g) * u).astype(BF16)

    return _mm("ffn_gate_up", [h], [w_gate, w_up], [(0, 0), (0, 1)], [], [_tile_out(TPAD, DFF, BF16, bn)], epi,
               bn=bn, vmem_mb=40)[0]


def _causal_head(q_ref, k_ref, v_ref, qi, qcols, kcols, scale, mask):
    rows = slice(qi * BQ, (qi + 1) * BQ)
    qb = q_ref[rows, qcols]
    pieces = []
    for kj in range(qi + 1):
        kb = k_ref[kj * BQ:(kj + 1) * BQ, kcols]
        s = lax.dot_general(qb, kb, (((1,), (1,)), ((), ())), preferred_element_type=F32) * scale
        if kj == qi:
            s = jnp.where(mask, s, -jnp.inf)
        pieces.append(s)
    m = functools.reduce(jnp.maximum, [jnp.max(s, axis=-1, keepdims=True) for s in pieces])
    l = None
    acc = None
    for kj, s in enumerate(pieces):
        p = jnp.exp(s - m)
        ps = jnp.sum(p, axis=-1, keepdims=True)
        pv = jnp.dot(p.astype(BF16), v_ref[kj * BQ:(kj + 1) * BQ, :], preferred_element_type=F32)
        l = ps if l is None else l + ps
        acc = pv if acc is None else acc + pv
    return acc / l


def _causal_mask():
    r = lax.broadcasted_iota(jnp.int32, (BQ, BQ), 0)
    c = lax.broadcasted_iota(jnp.int32, (BQ, BQ), 1)
    return c <= r


def _mla_prompt_attn(q, k, v):
    def kern(q_ref, k_ref, v_ref, o_ref):
        mask = _causal_mask()
        for qi in range(SEQ // BQ):
            o = _causal_head(q_ref, k_ref, v_ref, qi, slice(None), slice(None), SCALE_A, mask)
            o_ref[qi * BQ:(qi + 1) * BQ, :] = o.astype(BF16)

    return pl.pallas_call(
        kern,
        grid=(NB_PROMPT, HA),
        in_specs=[
            pl.BlockSpec((SEQ, QHEAD), lambda b, h: (b, h)),
            pl.BlockSpec((SEQ, QHEAD), lambda b, h: (b, h)),
            pl.BlockSpec((SEQ, VA), lambda b, h: (b, h)),
        ],
        out_specs=pl.BlockSpec((SEQ, VA), lambda b, h: (b, h)),
        out_shape=jax.ShapeDtypeStruct((TP, HA * VA), BF16),
        compiler_params=_cp(("arbitrary", "arbitrary"), 48),
        name="mla_prompt_attn",
    )(q, k, v)


def _lambda_value(lq1, lk1, lq2, lk2, lam_init):
    a = jnp.sum(lq1[...] * lk1[...], axis=-1, keepdims=True)
    b = jnp.sum(lq2[...] * lk2[...], axis=-1, keepdims=True)
    return jnp.exp(a) - jnp.exp(b) + lam_init


def _subln(diff, g, lam_init):
    y = diff * lax.rsqrt(jnp.mean(diff * diff, axis=-1, keepdims=True) + EPS)
    return y * g * (1.0 - lam_init)


def _diff_prompt_attn(qd, kd, vd, lams, g_subln, lam_init):
    def kern(q_ref, k_ref, v_ref, lq1, lk1, lq2, lk2, g_ref, o_ref):
        mask = _causal_mask()
        lam = _lambda_value(lq1, lk1, lq2, lk2, lam_init)
        for qi in range(SEQ // BQ):
            o0 = _causal_head(q_ref, k_ref, v_ref, qi, slice(0, HDB), slice(0, HDB), SCALE_B, mask)
            o1 = _causal_head(q_ref, k_ref, v_ref, qi, slice(HDB, 2 * HDB), slice(HDB, 2 * HDB), SCALE_B, mask)
            o_ref[qi * BQ:(qi + 1) * BQ, :] = _subln(o0 - lam * o1, g_ref[...], lam_init).astype(BF16)

    vec = pl.BlockSpec((1, HDB), lambda b, h: (0, 0))
    return pl.pallas_call(
        kern,
        grid=(NB_PROMPT, HB),
        in_specs=[
            pl.BlockSpec((SEQ, 2 * HDB), lambda b, h: (b, h)),
            pl.BlockSpec((SEQ, 2 * HDB), lambda b, h: (b, 0)),
            pl.BlockSpec((SEQ, 2 * HDB), lambda b, h: (b, 0)),
            vec, vec, vec, vec,
            pl.BlockSpec((1, 2 * HDB), lambda b, h: (0, 0)),
        ],
        out_specs=pl.BlockSpec((SEQ, 2 * HDB), lambda b, h: (b, h)),
        out_shape=jax.ShapeDtypeStruct((TP, HB * 2 * HDB), BF16),
        compiler_params=_cp(("arbitrary", "arbitrary"), 48),
        name="diff_prompt_attn",
    )(qd, kd, vd, *[x.reshape(1, HDB) for x in lams], g_subln.reshape(1, 2 * HDB))


def _page_spec(layer, shape, k, npages):
    return pl.BlockSpec((None, None) + shape,
                        lambda b, s, pt: (layer, pt[b * npages + s * PAGES_PER_STEP + k], 0, 0))


def _mla_decode(q_lat, q_pe, ckv_new, kpe_new, cache_ckv, cache_kpe_t, page_flat, layer, npages):
    nsteps = npages // PAGES_PER_STEP
    pp = PAGES_PER_STEP
    rows = pp * PAGE

    def kern(pt_ref, ql_ref, qp_ref, cn_ref, pn_ref, *rest):
        ckv_refs = rest[:pp]
        kpe_refs = rest[pp:2 * pp]
        o_ref = rest[2 * pp]
        kbuf, pbuf, m_ref, l_ref, acc_ref = rest[2 * pp + 1:]
        s = pl.program_id(1)

        @pl.when(s == 0)
        def _():
            cn = cn_ref[...]
            s0 = (jnp.sum(ql_ref[...].astype(F32) * cn, axis=-1, keepdims=True)
                  + jnp.sum(qp_ref[...].astype(F32) * pn_ref[...], axis=-1, keepdims=True)) * SCALE_A
            m_ref[...] = s0
            l_ref[...] = jnp.ones(l_ref.shape, F32)
            acc_ref[...] = jnp.broadcast_to(cn, acc_ref.shape)

        for k in range(pp):
            kbuf[k * PAGE:(k + 1) * PAGE, :] = ckv_refs[k][...].astype(BF16)
            pbuf[:, k * PAGE:(k + 1) * PAGE] = kpe_refs[k][...].astype(BF16)
        sc = lax.dot_general(ql_ref[...], kbuf[...], (((1,), (1,)), ((), ())), preferred_element_type=F32)
        sc = (sc + jnp.dot(qp_ref[...], pbuf[...], preferred_element_type=F32)) * SCALE_A
        m_old = m_ref[...]
        m_new = jnp.maximum(m_old, jnp.max(sc, axis=-1, keepdims=True))
        corr = jnp.exp(m_old - m_new)
        p = jnp.exp(sc - m_new)
        l_ref[...] = l_ref[...] * corr + jnp.sum(p, axis=-1, keepdims=True)
        acc_ref[...] = acc_ref[...] * corr + jnp.dot(p.astype(BF16), kbuf[...], preferred_element_type=F32)
        m_ref[...] = m_new

        @pl.when(s == nsteps - 1)
        def _():
            o_ref[...] = (acc_ref[...] / l_ref[...]).astype(BF16)

    in_specs = [
        pl.BlockSpec((None, HA, KVL), lambda b, s, pt: (b, 0, 0)),
        pl.BlockSpec((None, HA, ROPE_A), lambda b, s, pt: (b, 0, 0)),
        pl.BlockSpec((None, 1, KVL), lambda b, s, pt: (b, 0, 0)),
        pl.BlockSpec((None, 1, ROPE_A), lambda b, s, pt: (b, 0, 0)),
    ]
    in_specs += [_page_spec(layer, (PAGE, KVL), k, npages) for k in range(pp)]
    in_specs += [_page_spec(layer, (ROPE_A, PAGE), k, npages) for k in range(pp)]
    grid_spec = pltpu.PrefetchScalarGridSpec(
        num_scalar_prefetch=1,
        grid=(DB, nsteps),
        in_specs=in_specs,
        out_specs=pl.BlockSpec((None, HA, KVL), lambda b, s, pt: (b, 0, 0)),
        scratch_shapes=[pltpu.VMEM((rows, KVL), BF16), pltpu.VMEM((ROPE_A, rows), BF16),
                        pltpu.VMEM((HA, 1), F32), pltpu.VMEM((HA, 1), F32), pltpu.VMEM((HA, KVL), F32)],
    )
    return pl.pallas_call(
        kern,
        grid_spec=grid_spec,
        out_shape=jax.ShapeDtypeStruct((DB, HA, KVL), BF16),
        compiler_params=_cp(("arbitrary", "arbitrary"), 58),
        name="mla_decode",
    )(page_flat, q_lat, q_pe, ckv_new, kpe_new, *([cache_ckv] * pp), *([cache_kpe_t] * pp))


def _diff_decode(q_s, knew, vnew, cache_k, cache_v, page_flat, layer, npages, lams, g_subln, lam_init):
    nsteps = npages // PAGES_PER_STEP
    pp = PAGES_PER_STEP
    rows = pp * PAGE
    nrow = 2 * HB

    def kern(pt_ref, q_ref, kn_ref, vn_ref, lq1, lk1, lq2, lk2, g_ref, *rest):
        k_refs = rest[:pp]
        v_refs = rest[pp:2 * pp]
        o_ref = rest[2 * pp]
        kbuf, vbuf, m_ref, l_ref, acc_ref = rest[2 * pp + 1:]
        s = pl.program_id(1)
        q = q_ref[...]

        row = lax.broadcasted_iota(jnp.int32, (nrow, 1), 0)

        def scores(kmat_or_row, vpu):
            outs = []
            for mp in range(2):
                km = kmat_or_row[:, mp * HDB:(mp + 1) * HDB]
                if vpu:
                    outs.append(jnp.sum(q.astype(F32) * km, axis=-1, keepdims=True))
                else:
                    outs.append(lax.dot_general(q, km, (((1,), (1,)), ((), ())), preferred_element_type=F32))
            return jnp.where(row < HB, outs[0], outs[1]) * SCALE_B

        @pl.when(s == 0)
        def _():
            m_ref[...] = scores(kn_ref[...], True)
            l_ref[...] = jnp.ones(l_ref.shape, F32)
            acc_ref[...] = jnp.broadcast_to(vn_ref[...], acc_ref.shape)

        for k in range(pp):
            kbuf[k * PAGE:(k + 1) * PAGE, :] = k_refs[k][...].astype(BF16)
            vbuf[k * PAGE:(k + 1) * PAGE, :] = v_refs[k][...].astype(BF16)
        sc = scores(kbuf[...], False)
        m_old = m_ref[...]
        m_new = jnp.maximum(m_old, jnp.max(sc, axis=-1, keepdims=True))
        corr = jnp.exp(m_old - m_new)
        p = jnp.exp(sc - m_new)
        l_ref[...] = l_ref[...] * corr + jnp.sum(p, axis=-1, keepdims=True)
        acc_ref[...] = acc_ref[...] * corr + jnp.dot(p.astype(BF16), vbuf[...], preferred_element_type=F32)
        m_ref[...] = m_new

        @pl.when(s == nsteps - 1)
        def _():
            o = acc_ref[...] / l_ref[...]
            lam = _lambda_value(lq1, lk1, lq2, lk2, lam_init)
            o_ref[...] = _subln(o[0:HB, :] - lam * o[HB:nrow, :], g_ref[...], lam_init)

    vec = pl.BlockSpec((1, HDB), lambda b, s, pt: (0, 0))
    in_specs = [
        pl.BlockSpec((None, nrow, HDB), lambda b, s, pt: (b, 0, 0)),
        pl.BlockSpec((None, 1, 2 * HDB), lambda b, s, pt: (b, 0, 0)),
        pl.BlockSpec((None, 1, 2 * HDB), lambda b, s, pt: (b, 0, 0)),
        vec, vec, vec, vec,
        pl.BlockSpec((1, 2 * HDB), lambda b, s, pt: (0, 0)),
    ]
    in_specs += [_page_spec(layer, (PAGE, 2 * HDB), k, npages) for k in range(pp)]
    in_specs += [_page_spec(layer, (PAGE, 2 * HDB), k, npages) for k in range(pp)]
    grid_spec = pltpu.PrefetchScalarGridSpec(
        num_scalar_prefetch=1,
        grid=(DB, nsteps),
        in_specs=in_specs,
        out_specs=pl.BlockSpec((None, HB, 2 * HDB), lambda b, s, pt: (b, 0, 0)),
        scratch_shapes=[pltpu.VMEM((rows, 2 * HDB), BF16), pltpu.VMEM((rows, 2 * HDB), BF16),
                        pltpu.VMEM((nrow, 1), F32), pltpu.VMEM((nrow, 1), F32), pltpu.VMEM((nrow, 2 * HDB), F32)],
    )
    return pl.pallas_call(
        kern,
        grid_spec=grid_spec,
        out_shape=jax.ShapeDtypeStruct((DB, HB, 2 * HDB), F32),
        compiler_params=_cp(("arbitrary", "arbitrary"), 56),
        name="diff_decode",
    )(page_flat, q_s, knew, vnew, *[x.reshape(1, HDB) for x in lams], g_subln.reshape(1, 2 * HDB),
      *([cache_k] * pp), *([cache_v] * pp))


def _q_absorb(q, w_uk_flat):
    rb = TP // TS

    def kern(q_ref, w_ref, o_ref):
        o_ref[...] = lax.dot_general(q_ref[...], w_ref[...].astype(BF16), (((1,), (1,)), ((), ())),
                                     preferred_element_type=F32).astype(BF16)

    return pl.pallas_call(
        kern,
        grid=(HA,),
        in_specs=[pl.BlockSpec((TS, NOPE), lambda h: (rb, 2 * h)), pl.BlockSpec((KVL, NOPE), lambda h: (0, h))],
        out_specs=pl.BlockSpec((None, TS, KVL), lambda h: (h, 0, 0)),
        out_shape=jax.ShapeDtypeStruct((HA, TS, KVL), BF16),
        compiler_params=_cp(("arbitrary",), 16),
        name="q_absorb",
    )(q, w_uk_flat)


def _v_up_sample(o_lat_t, w_uv_flat):
    def kern(o_ref, w_ref, out_ref):
        out_ref[0:TS, :] = jnp.dot(o_ref[...], w_ref[...].astype(BF16), preferred_element_type=F32).astype(BF16)
        out_ref[TS:BM, :] = jnp.zeros((BM - TS, VA), BF16)

    return pl.pallas_call(
        kern,
        grid=(HA,),
        in_specs=[pl.BlockSpec((None, TS, KVL), lambda h: (h, 0, 0)), pl.BlockSpec((KVL, VA), lambda h: (0, h))],
        out_specs=pl.BlockSpec((BM, VA), lambda h: (0, h)),
        out_shape=jax.ShapeDtypeStruct((BM, HA * VA), BF16),
        compiler_params=_cp(("arbitrary",), 16),
        name="v_up_sample",
    )(o_lat_t, w_uv_flat)


def _route_meta(idx):
    row = jnp.arange(TPAD, dtype=jnp.int32)
    valid = row < TP + TS
    e = jnp.where(valid[:, None], idx, NE).reshape(-1)
    onehot = (e[:, None] == jnp.arange(NE, dtype=jnp.int32)[None, :]).astype(jnp.int32)
    csum = jnp.cumsum(onehot, axis=0)
    rank = jnp.sum((csum - onehot) * onehot, axis=1)
    counts = csum[-1]
    padded = ((counts + BMO - 1) // BMO) * BMO
    gend = jnp.cumsum(padded)
    gstart = gend - padded
    is_real = e < NE
    pos = jnp.where(is_real, gstart[jnp.minimum(e, NE - 1)] + rank, RPAD).astype(jnp.int32)
    token = jnp.repeat(row, 2)
    row_token = jnp.zeros((RPAD,), jnp.int32).at[pos].set(token, mode="drop")
    tile_start = jnp.arange(NT, dtype=jnp.int32) * BMO
    total = gend[-1]
    n_valid = (total // BMO).astype(jnp.int32)
    te = jnp.searchsorted(gend, tile_start, side="right").astype(jnp.int32)
    te_last = te[jnp.maximum(n_valid - 1, 0)]
    tile_valid = tile_start < total
    te = jnp.where(tile_valid, jnp.minimum(te, NE - 1), te_last)
    pos_tok = jnp.where(is_real, pos, 0)
    return row_token, te, n_valid.reshape(1), pos_tok


def _moe_gather(h32, row_token, n_valid):
    def kern(tok_ref, nv_ref, h_ref, o_ref, buf, sem):
        r = pl.program_id(0)

        @pl.when(r < nv_ref[0])
        def _():
            def start(k, c):
                t = tok_ref[r * BMO + k]
                pltpu.make_async_copy(h_ref.at[pl.ds(t, 1)], buf.at[pl.ds(k, 1)], sem).start()
                return c

            lax.fori_loop(0, BMO, start, 0)

            def wait(k, c):
                pltpu.make_async_copy(h_ref.at[pl.ds(0, 1)], buf.at[pl.ds(k, 1)], sem).wait()
                return c

            lax.fori_loop(0, BMO, wait, 0)
            o_ref[...] = buf[...].astype(BF16)

        @pl.when(r >= nv_ref[0])
        def _():
            o_ref[...] = jnp.zeros(o_ref.shape, BF16)

    grid_spec = pltpu.PrefetchScalarGridSpec(
        num_scalar_prefetch=2,
        grid=(NT,),
        in_specs=[pl.BlockSpec(memory_space=pl.ANY)],
        out_specs=pl.BlockSpec((BMO, D), lambda r, tok, nv: (r, 0)),
        scratch_shapes=[pltpu.VMEM((BMO, D), F32), pltpu.SemaphoreType.DMA(())],
    )
    return pl.pallas_call(
        kern,
        grid_spec=grid_spec,
        out_shape=jax.ShapeDtypeStruct((RPAD, D), BF16),
        compiler_params=_cp(("arbitrary",), 16),
        name="moe_gather",
    )(row_token, n_valid, h32)


def _moe_gate_up(xs, w_gate, w_up, te, n_valid):
    bf = 1024

    def kern(te_ref, nv_ref, x_ref, wg_ref, wu_ref, o_ref, wgb, wub):
        r = pl.program_id(1)
        prev = te_ref[jnp.maximum(r - 1, 0)]

        @pl.when((r == 0) | (te_ref[r] != prev))
        def _():
            wgb[...] = wg_ref[...].astype(BF16)
            wub[...] = wu_ref[...].astype(BF16)

        @pl.when(r < nv_ref[0])
        def _():
            x = x_ref[...]
            g = jnp.dot(x, wgb[...], preferred_element_type=F32)
            u = jnp.dot(x, wub[...], preferred_element_type=F32)
            o_ref[...] = (g * _sigmoid(g) * u).astype(BF16)

        @pl.when(r >= nv_ref[0])
        def _():
            o_ref[...] = jnp.zeros(o_ref.shape, BF16)

    grid_spec = pltpu.PrefetchScalarGridSpec(
        num_scalar_prefetch=2,
        grid=(DEXP // bf, NT),
        in_specs=[
            pl.BlockSpec((BMO, D), lambda f, r, te, nv: (jnp.minimum(r, nv[0] - 1), 0)),
            pl.BlockSpec((None, D, bf), lambda f, r, te, nv: (te[r], 0, f)),
            pl.BlockSpec((None, D, bf), lambda f, r, te, nv: (te[r], 0, f)),
        ],
        out_specs=pl.BlockSpec((BMO, bf), lambda f, r, te, nv: (r, f)),
        scratch_shapes=[pltpu.VMEM((D, bf), BF16), pltpu.VMEM((D, bf), BF16)],
    )
    return pl.pallas_call(
        kern,
        grid_spec=grid_spec,
        out_shape=jax.ShapeDtypeStruct((RPAD, DEXP), BF16),
        compiler_params=_cp(("arbitrary", "arbitrary"), 52),
        name="moe_gate_up",
    )(te, n_valid, xs, w_gate, w_up)


def _moe_down(a, w_down, te, n_valid):
    bn = 512

    def kern(te_ref, nv_ref, a_ref, w_ref, o_ref, wb):
        r = pl.program_id(1)
        prev = te_ref[jnp.maximum(r - 1, 0)]

        @pl.when((r == 0) | (te_ref[r] != prev))
        def _():
            wb[...] = w_ref[...].astype(BF16)

        @pl.when(r < nv_ref[0])
        def _():
            o_ref[...] = jnp.dot(a_ref[...], wb[...], preferred_element_type=F32)

        @pl.when(r >= nv_ref[0])
        def _():
            o_ref[...] = jnp.zeros(o_ref.shape, F32)

    grid_spec = pltpu.PrefetchScalarGridSpec(
        num_scalar_prefetch=2,
        grid=(D // bn, NT),
        in_specs=[
            pl.BlockSpec((BMO, DEXP), lambda n, r, te, nv: (jnp.minimum(r, nv[0] - 1), 0)),
            pl.BlockSpec((None, DEXP, bn), lambda n, r, te, nv: (te[r], 0, n)),
        ],
        out_specs=pl.BlockSpec((BMO, bn), lambda n, r, te, nv: (r, n)),
        scratch_shapes=[pltpu.VMEM((DEXP, bn), BF16)],
    )
    return pl.pallas_call(
        kern,
        grid_spec=grid_spec,
        out_shape=jax.ShapeDtypeStruct((RPAD, D), F32),
        compiler_params=_cp(("arbitrary", "arbitrary"), 56),
        name="moe_down",
    )(te, n_valid, a, w_down)


def _moe_combine(x_old, y, pos_tok, wts, gate):
    def kern(pos_ref, x_ref, gp, gs, wt_ref, y_ref, o_ref, buf0, buf1, sem):
        i = pl.program_id(0)

        def start(k, c):
            t = i * BM + k
            pltpu.make_async_copy(y_ref.at[pl.ds(pos_ref[2 * t], 1)], buf0.at[pl.ds(k, 1)], sem).start()
            pltpu.make_async_copy(y_ref.at[pl.ds(pos_ref[2 * t + 1], 1)], buf1.at[pl.ds(k, 1)], sem).start()
            return c

        lax.fori_loop(0, BM, start, 0)

        def wait(k, c):
            pltpu.make_async_copy(y_ref.at[pl.ds(0, 1)], buf0.at[pl.ds(k, 1)], sem).wait()
            pltpu.make_async_copy(y_ref.at[pl.ds(0, 1)], buf1.at[pl.ds(k, 1)], sem).wait()
            return c

        lax.fori_loop(0, BM, wait, 0)
        wt = wt_ref[...]
        mixed = wt[:, 0:1] * buf0[...] + wt[:, 1:2] * buf1[...]
        o_ref[...] = x_ref[...] + _sel_mod(i, gp, gs) * mixed

    grid_spec = pltpu.PrefetchScalarGridSpec(
        num_scalar_prefetch=1,
        grid=(NBLK,),
        in_specs=[
            pl.BlockSpec((BM, D), lambda i, pos: (i, 0)),
            pl.BlockSpec((1, 1, D), lambda i, pos: (jnp.minimum(i // BPB, NB_PROMPT - 1), 0, 0)),
            pl.BlockSpec((BM, D), lambda i, pos: (0, 0)),
            pl.BlockSpec((BM, LANES), lambda i, pos: (i, 0)),
            pl.BlockSpec(memory_space=pl.ANY),
        ],
        out_specs=pl.BlockSpec((BM, D), lambda i, pos: (i, 0)),
        scratch_shapes=[pltpu.VMEM((BM, D), F32), pltpu.VMEM((BM, D), F32), pltpu.SemaphoreType.DMA(())],
    )
    return pl.pallas_call(
        kern,
        grid_spec=grid_spec,
        out_shape=jax.ShapeDtypeStruct((TPAD, D), F32),
        compiler_params=_cp(("arbitrary",), 48),
        name="moe_combine",
    )(pos_tok, x_old, gate[0], gate[1], wts, y)


def _rope_tables(pos, rot_dim, period):
    half = rot_dim // 2
    lane = jnp.arange(LANES)
    within = lane % period
    active = within < rot_dim
    first = (within % rot_dim) < half
    inv_freq = THETA ** (-jnp.arange(half, dtype=F32) / half)
    ang = pos.astype(F32)[:, None] * inv_freq[None, :]
    cos = jnp.cos(ang)[:, within % half]
    sin = jnp.sin(ang)[:, within % half]
    c = jnp.where(active[None, :], cos, 1.0)
    s1 = jnp.where((active & first)[None, :], -sin, 0.0)
    s2 = jnp.where((active & ~first)[None, :], sin, 0.0)
    return c, s1, s2


def _split_mod(m):
    return m[:NB_PROMPT].reshape(NB_PROMPT, 1, D), jnp.pad(m[NB_PROMPT:NB_PROMPT + TS], ((0, BM - TS), (0, 0)))


def kernel(x_prompt, x_sample, c_prompt, c_sample, cache_mla_ckv, cache_mla_kpe, cache_diff_k, cache_diff_v, page_table, w_mod, b_mod, w_in, g_q_norm, w_uq, g_kv_norm, w_uk, w_uv, lambda_q1, lambda_k1, lambda_q2, lambda_k2, g_subln, w_branch_a, w_branch_b, w_out, w_ffn_gate, w_ffn_up, w_ffn_down, w_router, w_exp_gate, w_exp_up, w_exp_down, g_final):
    npages = page_table.shape[1]
    past_len = npages * cache_mla_ckv.shape[2]
    page_flat = page_table.reshape(-1).astype(jnp.int32)
    cache_kpe_t = jnp.swapaxes(cache_mla_kpe, 2, 3)

    x = jnp.concatenate([x_prompt.reshape(TP, D), x_sample.reshape(TS, D), jnp.zeros((BM - TS, D), F32)], axis=0)
    c_rows = NB_PROMPT + TS
    c_pad = (-c_rows) % 8
    c_all = jnp.concatenate([c_prompt, c_sample, jnp.zeros((c_pad, D), F32)], axis=0)
    mods = _adaln(c_all, w_mod, b_mod)

    pos = jnp.concatenate([jnp.tile(jnp.arange(SEQ, dtype=jnp.int32), NB_PROMPT),
                           jnp.full((TS,), past_len, jnp.int32), jnp.zeros((BM - TS,), jnp.int32)])
    tabs_a = _rope_tables(pos, ROPE_A, ROPE_A)
    tabs_b = _rope_tables(pos, ROT_B, LANES)

    rows_p = ([], [], [], [])
    rows_s = ([], [], [], [])
    for l in range(DEPTH):
        lam_init = 0.8 - 0.6 * math.exp(-0.3 * l)
        lams = (lambda_q1[l], lambda_k1[l], lambda_q2[l], lambda_k2[l])
        m6 = [_split_mod(mods[l, :, k * D:(k + 1) * D]) for k in range(N_MOD)]
        shift_m, scale_m, gate_m, shift_f, scale_f, gate_f = m6

        wl = w_in[l]
        w_qd = wl[:, OFF_QB:OFF_KB]
        w_g = wl[:, OFF_GA:]
        w_small = jnp.concatenate([wl[:, :OFF_KPE], wl[:, OFF_KB:OFF_GA], wl[:, OFF_KPE:OFF_QB],
                                   jnp.zeros((D, LANES - ROPE_A), F32)], axis=1)
        wq = w_uq[l].reshape(QL, HA, NOPE + ROPE_A)
        w_uq_pad = jnp.concatenate([wq, jnp.zeros((QL, HA, QHEAD - NOPE - ROPE_A), F32)], axis=2).reshape(QL, HA * QHEAD)
        w_uk_flat = w_uk[l].reshape(KVL, HA * NOPE)
        w_uv_flat = w_uv[l].reshape(KVL, HA * VA)

        h = _ln_mod(x, scale_m, shift_m)
        qd = _proj_qd(h, w_qd, tabs_b)
        gates = _proj_gates(h, w_g)
        cqn, ckv32, ckvbf, kd32, kdbf, vd32, vdbf, kpe32, kpebf = _proj_small(
            h, w_small, g_q_norm[l], g_kv_norm[l], tabs_a, tabs_b)
        q = _q_up(cqn, w_uq_pad, tabs_a)
        kfull, v = _kv_up(ckvbf, kpebf, w_uk_flat, w_uv_flat)

        o_a = _mla_prompt_attn(q, kfull, v)
        o_b = _diff_prompt_attn(qd, kdbf, vdbf, lams, g_subln[l], lam_init)

        q_lat = jnp.transpose(_q_absorb(q, w_uk_flat), (1, 0, 2))
        q_s = q[TP:TP + TS].reshape(TS, HA, QHEAD)
        o_lat = _mla_decode(q_lat, q_s[:, :, NOPE:NOPE + ROPE_A], ckv32[TP:TP + TS].reshape(TS, 1, KVL),
                            kpe32[TP:TP + TS, :ROPE_A].reshape(TS, 1, ROPE_A), cache_mla_ckv, cache_kpe_t,
                            page_flat, l, npages)
        o_a_s = _v_up_sample(jnp.transpose(o_lat, (1, 0, 2)), w_uv_flat)

        qd_s = qd[TP:TP + TS].reshape(TS, HB, 2, HDB).transpose(0, 2, 1, 3).reshape(TS, 2 * HB, HDB)
        o_b_s = _diff_decode(qd_s, kd32[TP:TP + TS].reshape(TS, 1, 2 * HDB), vd32[TP:TP + TS].reshape(TS, 1, 2 * HDB),
                             cache_diff_k, cache_diff_v, page_flat, l, npages, lams, g_subln[l], lam_init)
        o_b_s = jnp.pad(o_b_s.astype(BF16).reshape(TS, HB * 2 * HDB), ((0, BM - TS), (0, 0)))

        mixed = _merge((o_a, o_a_s), (o_b, o_b_s), (w_branch_a, l), (w_branch_b, l), gates)
        x = _residual_mm("attn_out", mixed, (w_out, l), x, gate_m, 1024, 40)

        if l % 2 == 0:
            h2 = _ln_mod(x, scale_f, shift_f)
            a = _ffn_gate_up(h2, w_ffn_gate[l // 2], w_ffn_up[l // 2])
            x = _residual_mm("ffn_down", a, w_ffn_down[l // 2], x, gate_f, 512, 52)
        else:
            w_router_pad = jnp.pad(w_router[l // 2], ((0, 0), (0, LANES - NE)))
            h2, idx, wts = _ln_mod(x, scale_f, shift_f, w_router_pad)
            row_token, te, n_valid, pos_tok = _route_meta(idx[:, :2])
            xs = _moe_gather(h2, row_token, n_valid)
            a = _moe_gate_up(xs, w_exp_gate[l // 2], w_exp_up[l // 2], te, n_valid)
            y = _moe_down(a, w_exp_down[l // 2], te, n_valid)
            x = _moe_combine(x, y, pos_tok, wts, gate_f)

        for buf, r in zip(rows_p, (ckv32, kpe32[:, :ROPE_A], kd32, vd32)):
            buf.append(r[:TP].reshape(NB_PROMPT, SEQ, r.shape[1]))
        for buf, r in zip(rows_s, (ckv32, kpe32[:, :ROPE_A], kd32, vd32)):
            buf.append(r[TP:TP + TS].reshape(TS, 1, r.shape[1]))

    y_prompt = _final_norm(x, g_final, NBP, 0, TP).reshape(NB_PROMPT, SEQ, D)
    y_sample = _final_norm(x, g_final, 1, NBP, BM)[:TS].reshape(TS, 1, D)
    return (y_prompt, y_sample,
            jnp.stack(rows_p[0]), jnp.stack(rows_p[1]), jnp.stack(rows_p[2]), jnp.stack(rows_p[3]),
            jnp.stack(rows_s[0]), jnp.stack(rows_s[1]), jnp.stack(rows_s[2]), jnp.stack(rows_s[3]))
```

```python
import functools
import math

import jax
import jax.numpy as jnp
from jax import lax
from jax.experimental import pallas as pl
from jax.experimental.pallas import tpu as pltpu

F32 = jnp.float32
BF16 = jnp.bfloat16

D = 2048
NB_PROMPT = 4
SEQ = 2048
DEPTH = 2
DB = 128
PAGE = 128
HA = 16
QL = 512
KVL = 512
NOPE = 128
ROPE_A = 64
VA = 128
HB = 8
HDB = 128
ROT_B = 32
DFF = 5632
NE = 8
DEXP = 7168
THETA = 500000.0
EPS = 1e-6
N_MOD = 6
SCALE_A = (NOPE + ROPE_A) ** -0.5
SCALE_B = HDB ** -0.5
LOG2E = math.log2(math.e)
OFF_CKV = QL
OFF_KPE = OFF_CKV + KVL
OFF_QB = OFF_KPE + ROPE_A
OFF_KB = OFF_QB + HB * 2 * HDB
OFF_VB = OFF_KB + 2 * HDB
OFF_GA = OFF_VB + 2 * HDB
OFF_GB = OFF_GA + D

LANES = 128
QHEAD = 2 * LANES

BM = 512
TP = NB_PROMPT * SEQ
TS = DB
TPAD = TP + BM
NBP = TP // BM
NBLK = NBP + 1
BPB = SEQ // BM
BQ = 512
PAGES_PER_STEP = 64
BMO = 256
DMA_UNROLL = 8
RPAD = ((2 * (TP + TS) + NE * BMO + BMO - 1) // BMO) * BMO
NT = RPAD // BMO
MIB = 1024 * 1024


def _cp(sem, vmem_mb):
    return pltpu.CompilerParams(dimension_semantics=sem, vmem_limit_bytes=int(vmem_mb * MIB))


def _sigmoid(x):
    return 1.0 / (1.0 + jnp.exp(-x))


def _rope(x, c, s1, s2, half):
    return x * c + pltpu.roll(x, LANES - half, 1) * s1 + pltpu.roll(x, half, 1) * s2


def _sel_mod(i, p_ref, s_ref):
    return jnp.where(i < NBP, p_ref[0], s_ref[...])


def _adaln(c_all, w_mod, b_mod):
    rows = c_all.shape[0]
    bn = 1024
    n = w_mod.shape[2]

    def kern(c_ref, w_ref, b_ref, o_ref):
        c = c_ref[...]
        a = (c * _sigmoid(c)).astype(BF16)
        o_ref[...] = jnp.dot(a, w_ref[...].astype(BF16), preferred_element_type=F32) + b_ref[...]

    return pl.pallas_call(
        kern,
        grid=(DEPTH, n // bn),
        in_specs=[
            pl.BlockSpec((rows, D), lambda l, j: (0, 0)),
            pl.BlockSpec((None, D, bn), lambda l, j: (l, 0, j)),
            pl.BlockSpec((None, 1, bn), lambda l, j: (l, 0, j)),
        ],
        out_specs=pl.BlockSpec((None, rows, bn), lambda l, j: (l, 0, j)),
        out_shape=jax.ShapeDtypeStruct((DEPTH, rows, n), F32),
        compiler_params=_cp(("arbitrary", "arbitrary"), 40),
        name="adaln",
    )(c_all, w_mod, b_mod.reshape(DEPTH, 1, n))


def _mod_specs():
    return [
        pl.BlockSpec((1, 1, D), lambda i: (jnp.minimum(i // BPB, NB_PROMPT - 1), 0, 0)),
        pl.BlockSpec((BM, D), lambda i: (0, 0)),
    ]


def _ln_mod(x, scale, shift, w_router_pad=None):
    route = w_router_pad is not None

    def kern(x_ref, scp, scs, shp, shs, *rest):
        i = pl.program_id(0)
        xv = x_ref[...]
        xn = xv * lax.rsqrt(jnp.mean(xv * xv, axis=-1, keepdims=True) + EPS)
        h = xn * (1.0 + _sel_mod(i, scp, scs)) + _sel_mod(i, shp, shs)
        if not route:
            (h_ref,) = rest
            h_ref[...] = h.astype(BF16)
            return
        rw_ref, h_ref, idx_ref, wt_ref = rest
        h_ref[...] = h
        logits = jnp.dot(h, rw_ref[...], preferred_element_type=F32, precision=lax.Precision.HIGHEST)
        lane = lax.broadcasted_iota(jnp.int32, logits.shape, 1).astype(F32)
        neg = jnp.float32(-jnp.inf)
        lg = jnp.where(lane < NE, logits, neg)
        m1 = jnp.max(lg, axis=-1, keepdims=True)
        i1 = jnp.min(jnp.where(lg == m1, lane, float(LANES)), axis=-1, keepdims=True)
        lg2 = jnp.where(lane == i1, neg, lg)
        m2 = jnp.max(lg2, axis=-1, keepdims=True)
        i2 = jnp.min(jnp.where(lg2 == m2, lane, float(LANES)), axis=-1, keepdims=True)
        e = jnp.exp(m2 - m1)
        den = 1.0 + e
        idx_ref[...] = jnp.where(lane == 0, i1, jnp.where(lane == 1, i2, 0.0)).astype(jnp.int32)
        wt_ref[...] = jnp.where(lane == 0, 1.0 / den, jnp.where(lane == 1, e / den, 0.0))

    in_specs = [pl.BlockSpec((BM, D), lambda i: (i, 0))] + _mod_specs() + _mod_specs()
    args = [x, scale[0], scale[1], shift[0], shift[1]]
    out_specs = [pl.BlockSpec((BM, D), lambda i: (i, 0))]
    out_shape = [jax.ShapeDtypeStruct((TPAD, D), F32 if route else BF16)]
    if route:
        in_specs.append(pl.BlockSpec((D, LANES), lambda i: (0, 0)))
        args.append(w_router_pad)
        out_specs += [pl.BlockSpec((BM, LANES), lambda i: (i, 0))] * 2
        out_shape += [jax.ShapeDtypeStruct((TPAD, LANES), jnp.int32), jax.ShapeDtypeStruct((TPAD, LANES), F32)]
    res = pl.pallas_call(
        kern,
        grid=(NBLK,),
        in_specs=in_specs,
        out_specs=out_specs,
        out_shape=out_shape,
        compiler_params=_cp(("arbitrary",), 40),
        name="ln_mod_route" if route else "ln_mod",
    )(*args)
    return res if route else res[0]


def _final_norm(x, g, nblk, blk0, rows):
    def kern(x_ref, g_ref, o_ref):
        xv = x_ref[...]
        o_ref[...] = xv * lax.rsqrt(jnp.mean(xv * xv, axis=-1, keepdims=True) + EPS) * g_ref[...]

    return pl.pallas_call(
        kern,
        grid=(nblk,),
        in_specs=[pl.BlockSpec((BM, D), lambda i: (i + blk0, 0)), pl.BlockSpec((1, D), lambda i: (0, 0))],
        out_specs=pl.BlockSpec((BM, D), lambda i: (i, 0)),
        out_shape=jax.ShapeDtypeStruct((rows, D), F32),
        compiler_params=_cp(("arbitrary",), 32),
        name="final_norm",
    )(x, g.reshape(1, D))


def _mm(name, xs, ws, pairs, extras, outs, epilogue, *, bn, nrb=NBLK, vmem_mb=48):
    w_arrs = [w[0] if isinstance(w, tuple) else w for w in ws]
    w_layer = [w[1] if isinstance(w, tuple) else None for w in ws]
    w_kn = [a.shape[-2:] for a in w_arrs]
    n = w_kn[0][1]
    x_ops = []
    x_slots = []
    for x in xs:
        parts = x if isinstance(x, tuple) else (x,)
        x_slots.append((len(x_ops), len(parts)))
        x_ops.extend(parts)
    nx, nw, nex, no = len(x_ops), len(ws), len(extras), len(outs)

    def kern(*refs):
        x_refs = refs[:nx]
        w_refs = refs[nx:nx + nw]
        e_refs = refs[nx + nw:nx + nw + nex]
        o_refs = refs[nx + nw + nex:nx + nw + nex + no]
        wb_refs = refs[nx + nw + nex + no:]
        j = pl.program_id(0)
        i = pl.program_id(1)

        @pl.when(i == 0)
        def _():
            for w_ref, wb_ref in zip(w_refs, wb_refs):
                wb_ref[...] = w_ref[...].astype(BF16)

        def x_val(a):
            lo, cnt = x_slots[a]
            if cnt == 1:
                return x_refs[lo][...]
            return jnp.where(i < NBP, x_refs[lo][...], x_refs[lo + 1][...])

        accs = [jnp.dot(x_val(a), wb_refs[b][...], preferred_element_type=F32) for a, b in pairs]
        epilogue(accs, e_refs, o_refs, i, j)

    in_specs = []
    for x in xs:
        if isinstance(x, tuple):
            in_specs.append(pl.BlockSpec((BM, x[0].shape[1]), lambda j, i: (jnp.minimum(i, NBP - 1), 0)))
            in_specs.append(pl.BlockSpec((BM, x[1].shape[1]), lambda j, i: (0, 0)))
        else:
            in_specs.append(pl.BlockSpec((BM, x.shape[1]), lambda j, i: (i, 0)))
    w_mode = dict(pipeline_mode=pl.Buffered(1)) if n == bn else {}
    for (k_dim, _), layer in zip(w_kn, w_layer):
        if layer is None:
            in_specs.append(pl.BlockSpec((k_dim, bn), lambda j, i: (0, j), **w_mode))
        else:
            in_specs.append(pl.BlockSpec((None, k_dim, bn), lambda j, i, layer=layer: (layer, 0, j), **w_mode))
    in_specs += [pl.BlockSpec(bs, im) for _, bs, im in extras]
    res = pl.pallas_call(
        kern,
        grid=(n // bn, nrb),
        in_specs=in_specs,
        out_specs=[pl.BlockSpec(bs, im) for _, bs, im in outs],
        out_shape=[sd for sd, _, _ in outs],
        scratch_shapes=[pltpu.VMEM((k_dim, bn), BF16) for k_dim, _ in w_kn],
        compiler_params=_cp(("arbitrary", "arbitrary"), vmem_mb),
        name=name,
    )(*x_ops, *w_arrs, *[a for a, _, _ in extras])
    return res


def _row_extra(arr, width=None):
    width = arr.shape[1] if width is None else width
    return (arr, (BM, width), lambda j, i: (i, 0))


def _const_extra(arr):
    return (arr, arr.shape, lambda j, i: (0,) * arr.ndim)


def _tile_out(rows, cols, dtype, bn):
    return (jax.ShapeDtypeStruct((rows, cols), dtype), (BM, bn), lambda j, i: (i, j))


def _mod_extras(mod, bn):
    return [
        (mod[0], (1, 1, bn), lambda j, i: (jnp.minimum(i // BPB, NB_PROMPT - 1), 0, j)),
        (mod[1], (BM, bn), lambda j, i: (0, j)),
    ]


def _proj_qd(h, w_qd, tabs_b):
    bn = 1024

    def epi(accs, e, o, i, j):
        c, s1, s2 = e[0][...], e[1][...], e[2][...]
        acc = accs[0]
        for k in range(bn // LANES):
            sl = slice(k * LANES, (k + 1) * LANES)
            o[0][:, sl] = _rope(acc[:, sl], c, s1, s2, ROT_B // 2).astype(BF16)

    return _mm("proj_qd", [h], [w_qd], [(0, 0)], [_row_extra(t) for t in tabs_b],
               [_tile_out(TPAD, HB * 2 * HDB, BF16, bn)], epi, bn=bn, vmem_mb=40)[0]


def _proj_gates(h, w_g):
    bn = 1024

    def epi(accs, e, o, i, j):
        o[0][...] = accs[0]

    return _mm("proj_gates", [h], [w_g], [(0, 0)], [], [_tile_out(TPAD, 2 * D, F32, bn)], epi,
               bn=bn, vmem_mb=40)[0]


W_SMALL = QL + KVL + 2 * HDB + 2 * HDB + LANES


def _proj_small(h, w_small, g_q, g_kv, tabs_a, tabs_b):
    def rms(x, g):
        return x * lax.rsqrt(jnp.mean(x * x, axis=-1, keepdims=True) + EPS) * g

    def epi(accs, e, o, i, j):
        acc = accs[0]
        gq, gkv = e[0][...], e[1][...]
        ca, s1a, s2a = e[2][...], e[3][...], e[4][...]
        cb, s1b, s2b = e[5][...], e[6][...], e[7][...]
        o[0][...] = rms(acc[:, 0:QL], gq).astype(BF16)
        ckv = rms(acc[:, QL:QL + KVL], gkv)
        o[1][...] = ckv
        o[2][...] = ckv.astype(BF16)
        off = QL + KVL
        for k in range(2):
            sl = slice(k * LANES, (k + 1) * LANES)
            kd = _rope(acc[:, off + k * LANES:off + (k + 1) * LANES], cb, s1b, s2b, ROT_B // 2)
            o[3][:, sl] = kd
            o[4][:, sl] = kd.astype(BF16)
        off += 2 * HDB
        vd = acc[:, off:off + 2 * HDB]
        o[5][...] = vd
        o[6][...] = vd.astype(BF16)
        off += 2 * HDB
        kpe = _rope(acc[:, off:off + LANES], ca, s1a, s2a, ROPE_A // 2)
        o[7][...] = kpe
        o[8][...] = kpe.astype(BF16)

    def full(cols, dtype):
        return (jax.ShapeDtypeStruct((TPAD, cols), dtype), (BM, cols), lambda j, i: (i, 0))

    extras = [_const_extra(g_q.reshape(1, QL)), _const_extra(g_kv.reshape(1, KVL))]
    extras += [_row_extra(t) for t in tabs_a] + [_row_extra(t) for t in tabs_b]
    outs = [full(QL, BF16), full(KVL, F32), full(KVL, BF16), full(2 * HDB, F32), full(2 * HDB, BF16),
            full(2 * HDB, F32), full(2 * HDB, BF16), full(LANES, F32), full(LANES, BF16)]
    return _mm("proj_small", [h], [w_small], [(0, 0)], extras, outs, epi, bn=W_SMALL, vmem_mb=52)


def _q_up(cqn, w_uq_pad, tabs_a):
    bn = 1024

    def epi(accs, e, o, i, j):
        c, s1, s2 = e[0][...], e[1][...], e[2][...]
        acc = accs[0]
        for k in range(bn // QHEAD):
            lo = k * QHEAD
            o[0][:, lo:lo + LANES] = acc[:, lo:lo + LANES].astype(BF16)
            o[0][:, lo + LANES:lo + QHEAD] = _rope(acc[:, lo + LANES:lo + QHEAD], c, s1, s2,
                                                   ROPE_A // 2).astype(BF16)

    return _mm("q_up", [cqn], [w_uq_pad], [(0, 0)], [_row_extra(t) for t in tabs_a],
               [_tile_out(TPAD, HA * QHEAD, BF16, bn)], epi, bn=bn, vmem_mb=32)[0]


def _kv_up(ckv_bf, kpe_bf, w_uk_flat, w_uv_flat):
    bn = 512

    def epi(accs, e, o, i, j):
        kn, v = accs
        kpe = e[0][...]
        for k in range(bn // NOPE):
            o[0][:, k * QHEAD:k * QHEAD + LANES] = kn[:, k * NOPE:(k + 1) * NOPE].astype(BF16)
            o[0][:, k * QHEAD + LANES:(k + 1) * QHEAD] = kpe
        o[1][...] = v.astype(BF16)

    outs = [
        (jax.ShapeDtypeStruct((TPAD, HA * QHEAD), BF16), (BM, 2 * bn), lambda j, i: (i, j)),
        _tile_out(TPAD, HA * VA, BF16, bn),
    ]
    return _mm("kv_up", [ckv_bf], [w_uk_flat, w_uv_flat], [(0, 0), (0, 1)], [_row_extra(kpe_bf)], outs, epi,
               bn=bn, vmem_mb=32)


def _merge(o_a, o_b, w_ba, w_bb, gates):
    bn = 512
    ncb = D // bn

    def epi(accs, e, o, i, j):
        o[0][...] = (_sigmoid(e[0][...]) * accs[0] + _sigmoid(e[1][...]) * accs[1]).astype(BF16)

    extras = [(gates, (BM, bn), lambda j, i: (i, j)), (gates, (BM, bn), lambda j, i: (i, j + ncb))]
    return _mm("merge", [o_a, o_b], [w_ba, w_bb], [(0, 0), (1, 1)], extras, [_tile_out(TPAD, D, BF16, bn)], epi,
               bn=bn, vmem_mb=48)[0]


def _residual_mm(name, a, w, x_old, gate, bn, vmem_mb):
    def epi(accs, e, o, i, j):
        o[0][...] = e[0][...] + _sel_mod(i, e[1], e[2]) * accs[0]

    extras = [(x_old, (BM, bn), lambda j, i: (i, j))] + _mod_extras(gate, bn)
    return _mm(name, [a], [w], [(0, 0)], extras, [_tile_out(TPAD, D, F32, bn)], epi, bn=bn, vmem_mb=vmem_mb)[0]


def _ffn_gate_up(h, w_gate, w_up):
    bn = 512

    def epi(accs, e, o, i, j):
        g, u = accs
        o[0][...] = (g * _sigmoid(g) * u).astype(BF16)

    return _mm("ffn_gate_up", [h], [w_gate, w_up], [(0, 0), (0, 1)], [], [_tile_out(TPAD, DFF, BF16, bn)], epi,
               bn=bn, vmem_mb=40)[0]


def _causal_head(q_ref, k_ref, v_ref, qi, qcols, kcols, scale, mask):
    rows = slice(qi * BQ, (qi + 1) * BQ)
    qb = q_ref[rows, qcols]
    pieces = []
    for kj in range(qi + 1):
        kb = k_ref[kj * BQ:(kj + 1) * BQ, kcols]
        s = lax.dot_general(qb, kb, (((1,), (1,)), ((), ())), preferred_element_type=F32) * (scale * LOG2E)
        if kj == qi:
            s = jnp.where(mask, s, -jnp.inf)
        pieces.append(s)
    m = functools.reduce(jnp.maximum, [jnp.max(s, axis=-1, keepdims=True) for s in pieces])
    l = None
    acc = None
    for kj, s in enumerate(pieces):
        p = jnp.exp2(s - m)
        ps = jnp.sum(p, axis=-1, keepdims=True)
        pv = jnp.dot(p.astype(BF16), v_ref[kj * BQ:(kj + 1) * BQ, :], preferred_element_type=F32)
        l = ps if l is None else l + ps
        acc = pv if acc is None else acc + pv
    return acc / l


def _causal_mask():
    r = lax.broadcasted_iota(jnp.int32, (BQ, BQ), 0)
    c = lax.broadcasted_iota(jnp.int32, (BQ, BQ), 1)
    return c <= r


def _mla_prompt_attn(q, k, v):
    def kern(q_ref, k_ref, v_ref, o_ref):
        mask = _causal_mask()
        for qi in range(SEQ // BQ):
            o = _causal_head(q_ref, k_ref, v_ref, qi, slice(None), slice(None), SCALE_A, mask)
            o_ref[qi * BQ:(qi + 1) * BQ, :] = o.astype(BF16)

    return pl.pallas_call(
        kern,
        grid=(NB_PROMPT, HA),
        in_specs=[
            pl.BlockSpec((SEQ, QHEAD), lambda b, h: (b, h)),
            pl.BlockSpec((SEQ, QHEAD), lambda b, h: (b, h)),
            pl.BlockSpec((SEQ, VA), lambda b, h: (b, h)),
        ],
        out_specs=pl.BlockSpec((SEQ, VA), lambda b, h: (b, h)),
        out_shape=jax.ShapeDtypeStruct((TP, HA * VA), BF16),
        compiler_params=_cp(("arbitrary", "arbitrary"), 48),
        name="mla_prompt_attn",
    )(q, k, v)


def _lambda_value(lq1, lk1, lq2, lk2, lam_init):
    a = jnp.sum(lq1[...] * lk1[...], axis=-1, keepdims=True)
    b = jnp.sum(lq2[...] * lk2[...], axis=-1, keepdims=True)
    return jnp.exp(a) - jnp.exp(b) + lam_init


def _subln(diff, g, lam_init):
    y = diff * lax.rsqrt(jnp.mean(diff * diff, axis=-1, keepdims=True) + EPS)
    return y * g * (1.0 - lam_init)


def _diff_prompt_attn(qd, kd, vd, lams, g_subln, lam_init):
    def kern(q_ref, k_ref, v_ref, lq1, lk1, lq2, lk2, g_ref, o_ref):
        mask = _causal_mask()
        lam = _lambda_value(lq1, lk1, lq2, lk2, lam_init)
        for qi in range(SEQ // BQ):
            o0 = _causal_head(q_ref, k_ref, v_ref, qi, slice(0, HDB), slice(0, HDB), SCALE_B, mask)
            o1 = _causal_head(q_ref, k_ref, v_ref, qi, slice(HDB, 2 * HDB), slice(HDB, 2 * HDB), SCALE_B, mask)
            o_ref[qi * BQ:(qi + 1) * BQ, :] = _subln(o0 - lam * o1, g_ref[...], lam_init).astype(BF16)

    vec = pl.BlockSpec((1, HDB), lambda b, h: (0, 0))
    return pl.pallas_call(
        kern,
        grid=(NB_PROMPT, HB),
        in_specs=[
            pl.BlockSpec((SEQ, 2 * HDB), lambda b, h: (b, h)),
            pl.BlockSpec((SEQ, 2 * HDB), lambda b, h: (b, 0)),
            pl.BlockSpec((SEQ, 2 * HDB), lambda b, h: (b, 0)),
            vec, vec, vec, vec,
            pl.BlockSpec((1, 2 * HDB), lambda b, h: (0, 0)),
        ],
        out_specs=pl.BlockSpec((SEQ, 2 * HDB), lambda b, h: (b, h)),
        out_shape=jax.ShapeDtypeStruct((TP, HB * 2 * HDB), BF16),
        compiler_params=_cp(("arbitrary", "arbitrary"), 48),
        name="diff_prompt_attn",
    )(qd, kd, vd, *[x.reshape(1, HDB) for x in lams], g_subln.reshape(1, 2 * HDB))


def _page_spec(layer, shape, k, npages):
    return pl.BlockSpec((None, None) + shape,
                        lambda b, s, pt: (layer, pt[b * npages + s * PAGES_PER_STEP + k], 0, 0))


def _mla_decode(q_lat, q_pe, ckv_new, kpe_new, cache_ckv, cache_kpe_t, page_flat, layer, npages):
    nsteps = npages // PAGES_PER_STEP
    pp = PAGES_PER_STEP
    rows = pp * PAGE

    def kern(pt_ref, ql_ref, qp_ref, cn_ref, pn_ref, *rest):
        ckv_refs = rest[:pp]
        kpe_refs = rest[pp:2 * pp]
        o_ref = rest[2 * pp]
        kbuf, pbuf, m_ref, l_ref, acc_ref = rest[2 * pp + 1:]
        s = pl.program_id(1)

        @pl.when(s == 0)
        def _():
            cn = cn_ref[...]
            s0 = (jnp.sum(ql_ref[...].astype(F32) * cn, axis=-1, keepdims=True)
                  + jnp.sum(qp_ref[...].astype(F32) * pn_ref[...], axis=-1, keepdims=True)) * SCALE_A
            m_ref[...] = s0
            l_ref[...] = jnp.ones(l_ref.shape, F32)
            acc_ref[...] = jnp.broadcast_to(cn, acc_ref.shape)

        for k in range(pp):
            kbuf[k * PAGE:(k + 1) * PAGE, :] = ckv_refs[k][...].astype(BF16)
            pbuf[:, k * PAGE:(k + 1) * PAGE] = kpe_refs[k][...].astype(BF16)
        sc = lax.dot_general(ql_ref[...], kbuf[...], (((1,), (1,)), ((), ())), preferred_element_type=F32)
        sc = (sc + jnp.dot(qp_ref[...], pbuf[...], preferred_element_type=F32)) * SCALE_A
        m_old = m_ref[...]
        m_new = jnp.maximum(m_old, jnp.max(sc, axis=-1, keepdims=True))
        corr = jnp.exp(m_old - m_new)
        p = jnp.exp(sc - m_new)
        l_ref[...] = l_ref[...] * corr + jnp.sum(p, axis=-1, keepdims=True)
        acc_ref[...] = acc_ref[...] * corr + jnp.dot(p.astype(BF16), kbuf[...], preferred_element_type=F32)
        m_ref[...] = m_new

        @pl.when(s == nsteps - 1)
        def _():
            o_ref[...] = (acc_ref[...] / l_ref[...]).astype(BF16)

    in_specs = [
        pl.BlockSpec((None, HA, KVL), lambda b, s, pt: (b, 0, 0)),
        pl.BlockSpec((None, HA, ROPE_A), lambda b, s, pt: (b, 0, 0)),
        pl.BlockSpec((None, 1, KVL), lambda b, s, pt: (b, 0, 0)),
        pl.BlockSpec((None, 1, ROPE_A), lambda b, s, pt: (b, 0, 0)),
    ]
    in_specs += [_page_spec(layer, (PAGE, KVL), k, npages) for k in range(pp)]
    in_specs += [_page_spec(layer, (ROPE_A, PAGE), k, npages) for k in range(pp)]
    grid_spec = pltpu.PrefetchScalarGridSpec(
        num_scalar_prefetch=1,
        grid=(DB, nsteps),
        in_specs=in_specs,
        out_specs=pl.BlockSpec((None, HA, KVL), lambda b, s, pt: (b, 0, 0)),
        scratch_shapes=[pltpu.VMEM((rows, KVL), BF16), pltpu.VMEM((ROPE_A, rows), BF16),
                        pltpu.VMEM((HA, 1), F32), pltpu.VMEM((HA, 1), F32), pltpu.VMEM((HA, KVL), F32)],
    )
    return pl.pallas_call(
        kern,
        grid_spec=grid_spec,
        out_shape=jax.ShapeDtypeStruct((DB, HA, KVL), BF16),
        compiler_params=_cp(("arbitrary", "arbitrary"), 58),
        name="mla_decode",
    )(page_flat, q_lat, q_pe, ckv_new, kpe_new, *([cache_ckv] * pp), *([cache_kpe_t] * pp))


def _diff_decode(q_s, knew, vnew, cache_k, cache_v, page_flat, layer, npages, lams, g_subln, lam_init):
    nsteps = npages // PAGES_PER_STEP
    pp = PAGES_PER_STEP
    rows = pp * PAGE
    nrow = 2 * HB

    def kern(pt_ref, q_ref, kn_ref, vn_ref, lq1, lk1, lq2, lk2, g_ref, *rest):
        k_refs = rest[:pp]
        v_refs = rest[pp:2 * pp]
        o_ref = rest[2 * pp]
        kbuf, vbuf, m_ref, l_ref, acc_ref = rest[2 * pp + 1:]
        s = pl.program_id(1)
        q = q_ref[...]

        row = lax.broadcasted_iota(jnp.int32, (nrow, 1), 0)

        def scores(kmat_or_row, vpu):
            outs = []
            for mp in range(2):
                km = kmat_or_row[:, mp * HDB:(mp + 1) * HDB]
                if vpu:
                    outs.append(jnp.sum(q.astype(F32) * km, axis=-1, keepdims=True))
                else:
                    outs.append(lax.dot_general(q, km, (((1,), (1,)), ((), ())), preferred_element_type=F32))
            return jnp.where(row < HB, outs[0], outs[1]) * SCALE_B

        @pl.when(s == 0)
        def _():
            m_ref[...] = scores(kn_ref[...], True)
            l_ref[...] = jnp.ones(l_ref.shape, F32)
            acc_ref[...] = jnp.broadcast_to(vn_ref[...], acc_ref.shape)

        for k in range(pp):
            kbuf[k * PAGE:(k + 1) * PAGE, :] = k_refs[k][...].astype(BF16)
            vbuf[k * PAGE:(k + 1) * PAGE, :] = v_refs[k][...].astype(BF16)
        sc = scores(kbuf[...], False)
        m_old = m_ref[...]
        m_new = jnp.maximum(m_old, jnp.max(sc, axis=-1, keepdims=True))
        corr = jnp.exp(m_old - m_new)
        p = jnp.exp(sc - m_new)
        l_ref[...] = l_ref[...] * corr + jnp.sum(p, axis=-1, keepdims=True)
        acc_ref[...] = acc_ref[...] * corr + jnp.dot(p.astype(BF16), vbuf[...], preferred_element_type=F32)
        m_ref[...] = m_new

        @pl.when(s == nsteps - 1)
        def _():
            o = acc_ref[...] / l_ref[...]
            lam = _lambda_value(lq1, lk1, lq2, lk2, lam_init)
            o_ref[...] = _subln(o[0:HB, :] - lam * o[HB:nrow, :], g_ref[...], lam_init)

    vec = pl.BlockSpec((1, HDB), lambda b, s, pt: (0, 0))
    in_specs = [
        pl.BlockSpec((None, nrow, HDB), lambda b, s, pt: (b, 0, 0)),
        pl.BlockSpec((None, 1, 2 * HDB), lambda b, s, pt: (b, 0, 0)),
        pl.BlockSpec((None, 1, 2 * HDB), lambda b, s, pt: (b, 0, 0)),
        vec, vec, vec, vec,
        pl.BlockSpec((1, 2 * HDB), lambda b, s, pt: (0, 0)),
    ]
    in_specs += [_page_spec(layer, (PAGE, 2 * HDB), k, npages) for k in range(pp)]
    in_specs += [_page_spec(layer, (PAGE, 2 * HDB), k, npages) for k in range(pp)]
    grid_spec = pltpu.PrefetchScalarGridSpec(
        num_scalar_prefetch=1,
        grid=(DB, nsteps),
        in_specs=in_specs,
        out_specs=pl.BlockSpec((None, HB, 2 * HDB), lambda b, s, pt: (b, 0, 0)),
        scratch_shapes=[pltpu.VMEM((rows, 2 * HDB), BF16), pltpu.VMEM((rows, 2 * HDB), BF16),
                        pltpu.VMEM((nrow, 1), F32), pltpu.VMEM((nrow, 1), F32), pltpu.VMEM((nrow, 2 * HDB), F32)],
    )
    return pl.pallas_call(
        kern,
        grid_spec=grid_spec,
        out_shape=jax.ShapeDtypeStruct((DB, HB, 2 * HDB), F32),
        compiler_params=_cp(("arbitrary", "arbitrary"), 56),
        name="diff_decode",
    )(page_flat, q_s, knew, vnew, *[x.reshape(1, HDB) for x in lams], g_subln.reshape(1, 2 * HDB),
      *([cache_k] * pp), *([cache_v] * pp))


def _q_absorb(q, w_uk_flat):
    rb = TP // TS

    def kern(q_ref, w_ref, o_ref):
        o_ref[...] = lax.dot_general(q_ref[...], w_ref[...].astype(BF16), (((1,), (1,)), ((), ())),
                                     preferred_element_type=F32).astype(BF16)

    return pl.pallas_call(
        kern,
        grid=(HA,),
        in_specs=[pl.BlockSpec((TS, NOPE), lambda h: (rb, 2 * h)), pl.BlockSpec((KVL, NOPE), lambda h: (0, h))],
        out_specs=pl.BlockSpec((None, TS, KVL), lambda h: (h, 0, 0)),
        out_shape=jax.ShapeDtypeStruct((HA, TS, KVL), BF16),
        compiler_params=_cp(("arbitrary",), 16),
        name="q_absorb",
    )(q, w_uk_flat)


def _v_up_sample(o_lat_t, w_uv_flat):
    def kern(o_ref, w_ref, out_ref):
        out_ref[0:TS, :] = jnp.dot(o_ref[...], w_ref[...].astype(BF16), preferred_element_type=F32).astype(BF16)
        out_ref[TS:BM, :] = jnp.zeros((BM - TS, VA), BF16)

    return pl.pallas_call(
        kern,
        grid=(HA,),
        in_specs=[pl.BlockSpec((None, TS, KVL), lambda h: (h, 0, 0)), pl.BlockSpec((KVL, VA), lambda h: (0, h))],
        out_specs=pl.BlockSpec((BM, VA), lambda h: (0, h)),
        out_shape=jax.ShapeDtypeStruct((BM, HA * VA), BF16),
        compiler_params=_cp(("arbitrary",), 16),
        name="v_up_sample",
    )(o_lat_t, w_uv_flat)


def _route_meta(idx):
    row = jnp.arange(TPAD, dtype=jnp.int32)
    valid = row < TP + TS
    e = jnp.where(valid[:, None], idx, NE).reshape(-1)
    onehot = (e[:, None] == jnp.arange(NE, dtype=jnp.int32)[None, :]).astype(jnp.int32)
    csum = jnp.cumsum(onehot, axis=0)
    rank = jnp.sum((csum - onehot) * onehot, axis=1)
    counts = csum[-1]
    padded = ((counts + BMO - 1) // BMO) * BMO
    gend = jnp.cumsum(padded)
    gstart = gend - padded
    is_real = e < NE
    pos = jnp.where(is_real, gstart[jnp.minimum(e, NE - 1)] + rank, RPAD).astype(jnp.int32)
    token = jnp.repeat(row, 2)
    row_token = jnp.zeros((RPAD,), jnp.int32).at[pos].set(token, mode="drop")
    tile_start = jnp.arange(NT, dtype=jnp.int32) * BMO
    total = gend[-1]
    n_valid = (total // BMO).astype(jnp.int32)
    te = jnp.searchsorted(gend, tile_start, side="right").astype(jnp.int32)
    te_last = te[jnp.maximum(n_valid - 1, 0)]
    tile_valid = tile_start < total
    te = jnp.where(tile_valid, jnp.minimum(te, NE - 1), te_last)
    pos_tok = jnp.where(is_real, pos, 0)
    return row_token, te, n_valid.reshape(1), pos_tok


def _moe_gather(h32, row_token, n_valid):
    def kern(tok_ref, nv_ref, h_ref, o_ref, buf, sem):
        r = pl.program_id(0)
        nv = nv_ref[0]
        slot = r % 2

        def row_copy(src_row, tile_slot, k):
            return pltpu.make_async_copy(h_ref.at[pl.ds(src_row, 1)], buf.at[tile_slot, pl.ds(k, 1)],
                                         sem.at[tile_slot])

        def issue(tile, tile_slot):
            def start(k, c):
                row_copy(tok_ref[tile * BMO + k], tile_slot, k).start()
                return c

            lax.fori_loop(0, BMO, start, 0, unroll=DMA_UNROLL)

        @pl.when((r == 0) & (nv > 0))
        def _():
            issue(0, 0)

        @pl.when(r + 1 < nv)
        def _():
            issue(r + 1, 1 - slot)

        @pl.when(r < nv)
        def _():
            def wait(k, c):
                row_copy(0, slot, k).wait()
                return c

            lax.fori_loop(0, BMO, wait, 0, unroll=DMA_UNROLL)
            o_ref[...] = buf[slot].astype(BF16)

        @pl.when(r >= nv)
        def _():
            o_ref[...] = jnp.zeros(o_ref.shape, BF16)

    grid_spec = pltpu.PrefetchScalarGridSpec(
        num_scalar_prefetch=2,
        grid=(NT,),
        in_specs=[pl.BlockSpec(memory_space=pl.ANY)],
        out_specs=pl.BlockSpec((BMO, D), lambda r, tok, nv: (r, 0)),
        scratch_shapes=[pltpu.VMEM((2, BMO, D), F32), pltpu.SemaphoreType.DMA((2,))],
    )
    return pl.pallas_call(
        kern,
        grid_spec=grid_spec,
        out_shape=jax.ShapeDtypeStruct((RPAD, D), BF16),
        compiler_params=_cp(("arbitrary",), 16),
        name="moe_gather",
    )(row_token, n_valid, h32)


def _moe_gate_up(xs, w_gate, w_up, te, n_valid):
    bf = 1024

    def kern(te_ref, nv_ref, x_ref, wg_ref, wu_ref, o_ref, wgb, wub):
        r = pl.program_id(1)
        prev = te_ref[jnp.maximum(r - 1, 0)]

        @pl.when((r == 0) | (te_ref[r] != prev))
        def _():
            wgb[...] = wg_ref[...].astype(BF16)
            wub[...] = wu_ref[...].astype(BF16)

        @pl.when(r < nv_ref[0])
        def _():
            x = x_ref[...]
            g = jnp.dot(x, wgb[...], preferred_element_type=F32)
            u = jnp.dot(x, wub[...], preferred_element_type=F32)
            o_ref[...] = (g * _sigmoid(g) * u).astype(BF16)

        @pl.when(r >= nv_ref[0])
        def _():
            o_ref[...] = jnp.zeros(o_ref.shape, BF16)

    grid_spec = pltpu.PrefetchScalarGridSpec(
        num_scalar_prefetch=2,
        grid=(DEXP // bf, NT),
        in_specs=[
            pl.BlockSpec((BMO, D), lambda f, r, te, nv: (jnp.minimum(r, nv[0] - 1), 0)),
            pl.BlockSpec((None, D, bf), lambda f, r, te, nv: (te[r], 0, f)),
            pl.BlockSpec((None, D, bf), lambda f, r, te, nv: (te[r], 0, f)),
        ],
        out_specs=pl.BlockSpec((BMO, bf), lambda f, r, te, nv: (r, f)),
        scratch_shapes=[pltpu.VMEM((D, bf), BF16), pltpu.VMEM((D, bf), BF16)],
    )
    return pl.pallas_call(
        kern,
        grid_spec=grid_spec,
        out_shape=jax.ShapeDtypeStruct((RPAD, DEXP), BF16),
        compiler_params=_cp(("arbitrary", "arbitrary"), 52),
        name="moe_gate_up",
    )(te, n_valid, xs, w_gate, w_up)


def _moe_down(a, w_down, te, n_valid):
    bn = 512

    def kern(te_ref, nv_ref, a_ref, w_ref, o_ref, wb):
        r = pl.program_id(1)
        prev = te_ref[jnp.maximum(r - 1, 0)]

        @pl.when((r == 0) | (te_ref[r] != prev))
        def _():
            wb[...] = w_ref[...].astype(BF16)

        @pl.when(r < nv_ref[0])
        def _():
            o_ref[...] = jnp.dot(a_ref[...], wb[...], preferred_element_type=F32)

        @pl.when(r >= nv_ref[0])
        def _():
            o_ref[...] = jnp.zeros(o_ref.shape, F32)

    grid_spec = pltpu.PrefetchScalarGridSpec(
        num_scalar_prefetch=2,
        grid=(D // bn, NT),
        in_specs=[
            pl.BlockSpec((BMO, DEXP), lambda n, r, te, nv: (jnp.minimum(r, nv[0] - 1), 0)),
            pl.BlockSpec((None, DEXP, bn), lambda n, r, te, nv: (te[r], 0, n)),
        ],
        out_specs=pl.BlockSpec((BMO, bn), lambda n, r, te, nv: (r, n)),
        scratch_shapes=[pltpu.VMEM((DEXP, bn), BF16)],
    )
    return pl.pallas_call(
        kern,
        grid_spec=grid_spec,
        out_shape=jax.ShapeDtypeStruct((RPAD, D), F32),
        compiler_params=_cp(("arbitrary", "arbitrary"), 56),
        name="moe_down",
    )(te, n_valid, a, w_down)


def _moe_combine(x_old, y, pos_tok, wts, gate):
    def kern(pos_ref, x_ref, gp, gs, wt_ref, y_ref, o_ref, buf0, buf1, sem):
        i = pl.program_id(0)
        slot = i % 2

        def row_copies(p0, p1, blk_slot, k):
            return (pltpu.make_async_copy(y_ref.at[pl.ds(p0, 1)], buf0.at[blk_slot, pl.ds(k, 1)], sem.at[blk_slot]),
                    pltpu.make_async_copy(y_ref.at[pl.ds(p1, 1)], buf1.at[blk_slot, pl.ds(k, 1)], sem.at[blk_slot]))

        def issue(blk, blk_slot):
            def start(k, c):
                t = blk * BM + k
                c0, c1 = row_copies(pos_ref[2 * t], pos_ref[2 * t + 1], blk_slot, k)
                c0.start()
                c1.start()
                return c

            lax.fori_loop(0, BM, start, 0, unroll=DMA_UNROLL)

        @pl.when(i == 0)
        def _():
            issue(0, 0)

        @pl.when(i + 1 < NBLK)
        def _():
            issue(i + 1, 1 - slot)

        def wait(k, c):
            c0, c1 = row_copies(0, 0, slot, k)
            c0.wait()
            c1.wait()
            return c

        lax.fori_loop(0, BM, wait, 0, unroll=DMA_UNROLL)
        wt = wt_ref[...]
        mixed = wt[:, 0:1] * buf0[slot] + wt[:, 1:2] * buf1[slot]
        o_ref[...] = x_ref[...] + _sel_mod(i, gp, gs) * mixed

    grid_spec = pltpu.PrefetchScalarGridSpec(
        num_scalar_prefetch=1,
        grid=(NBLK,),
        in_specs=[
            pl.BlockSpec((BM, D), lambda i, pos: (i, 0)),
            pl.BlockSpec((1, 1, D), lambda i, pos: (jnp.minimum(i // BPB, NB_PROMPT - 1), 0, 0)),
            pl.BlockSpec((BM, D), lambda i, pos: (0, 0)),
            pl.BlockSpec((BM, LANES), lambda i, pos: (i, 0)),
            pl.BlockSpec(memory_space=pl.ANY),
        ],
        out_specs=pl.BlockSpec((BM, D), lambda i, pos: (i, 0)),
        scratch_shapes=[pltpu.VMEM((2, BM, D), F32), pltpu.VMEM((2, BM, D), F32), pltpu.SemaphoreType.DMA((2,))],
    )
    return pl.pallas_call(
        kern,
        grid_spec=grid_spec,
        out_shape=jax.ShapeDtypeStruct((TPAD, D), F32),
        compiler_params=_cp(("arbitrary",), 48),
        name="moe_combine",
    )(pos_tok, x_old, gate[0], gate[1], wts, y)


def _rope_tables(pos, rot_dim, period):
    half = rot_dim // 2
    lane = jnp.arange(LANES)
    within = lane % period
    active = within < rot_dim
    first = (within % rot_dim) < half
    inv_freq = THETA ** (-jnp.arange(half, dtype=F32) / half)
    ang = pos.astype(F32)[:, None] * inv_freq[None, :]
    cos = jnp.cos(ang)[:, within % half]
    sin = jnp.sin(ang)[:, within % half]
    c = jnp.where(active[None, :], cos, 1.0)
    s1 = jnp.where((active & first)[None, :], -sin, 0.0)
    s2 = jnp.where((active & ~first)[None, :], sin, 0.0)
    return c, s1, s2


def _split_mod(m):
    return m[:NB_PROMPT].reshape(NB_PROMPT, 1, D), jnp.pad(m[NB_PROMPT:NB_PROMPT + TS], ((0, BM - TS), (0, 0)))


def kernel(x_prompt, x_sample, c_prompt, c_sample, cache_mla_ckv, cache_mla_kpe, cache_diff_k, cache_diff_v, page_table, w_mod, b_mod, w_in, g_q_norm, w_uq, g_kv_norm, w_uk, w_uv, lambda_q1, lambda_k1, lambda_q2, lambda_k2, g_subln, w_branch_a, w_branch_b, w_out, w_ffn_gate, w_ffn_up, w_ffn_down, w_router, w_exp_gate, w_exp_up, w_exp_down, g_final):
    npages = page_table.shape[1]
    past_len = npages * cache_mla_ckv.shape[2]
    page_flat = page_table.reshape(-1).astype(jnp.int32)
    cache_kpe_t = jnp.swapaxes(cache_mla_kpe, 2, 3)

    x = jnp.concatenate([x_prompt.reshape(TP, D), x_sample.reshape(TS, D), jnp.zeros((BM - TS, D), F32)], axis=0)
    c_rows = NB_PROMPT + TS
    c_pad = (-c_rows) % 8
    c_all = jnp.concatenate([c_prompt, c_sample, jnp.zeros((c_pad, D), F32)], axis=0)
    mods = _adaln(c_all, w_mod, b_mod)

    pos = jnp.concatenate([jnp.tile(jnp.arange(SEQ, dtype=jnp.int32), NB_PROMPT),
                           jnp.full((TS,), past_len, jnp.int32), jnp.zeros((BM - TS,), jnp.int32)])
    tabs_a = _rope_tables(pos, ROPE_A, ROPE_A)
    tabs_b = _rope_tables(pos, ROT_B, LANES)

    rows_p = ([], [], [], [])
    rows_s = ([], [], [], [])
    for l in range(DEPTH):
        lam_init = 0.8 - 0.6 * math.exp(-0.3 * l)
        lams = (lambda_q1[l], lambda_k1[l], lambda_q2[l], lambda_k2[l])
        m6 = [_split_mod(mods[l, :, k * D:(k + 1) * D]) for k in range(N_MOD)]
        shift_m, scale_m, gate_m, shift_f, scale_f, gate_f = m6

        wl = w_in[l]
        w_qd = wl[:, OFF_QB:OFF_KB]
        w_g = wl[:, OFF_GA:]
        w_small = jnp.concatenate([wl[:, :OFF_KPE], wl[:, OFF_KB:OFF_GA], wl[:, OFF_KPE:OFF_QB],
                                   jnp.zeros((D, LANES - ROPE_A), F32)], axis=1)
        wq = w_uq[l].reshape(QL, HA, NOPE + ROPE_A)
        w_uq_pad = jnp.concatenate([wq, jnp.zeros((QL, HA, QHEAD - NOPE - ROPE_A), F32)], axis=2).reshape(QL, HA * QHEAD)
        w_uk_flat = w_uk[l].reshape(KVL, HA * NOPE)
        w_uv_flat = w_uv[l].reshape(KVL, HA * VA)

        h = _ln_mod(x, scale_m, shift_m)
        qd = _proj_qd(h, w_qd, tabs_b)
        gates = _proj_gates(h, w_g)
        cqn, ckv32, ckvbf, kd32, kdbf, vd32, vdbf, kpe32, kpebf = _proj_small(
            h, w_small, g_q_norm[l], g_kv_norm[l], tabs_a, tabs_b)
        q = _q_up(cqn, w_uq_pad, tabs_a)
        kfull, v = _kv_up(ckvbf, kpebf, w_uk_flat, w_uv_flat)

        o_a = _mla_prompt_attn(q, kfull, v)
        o_b = _diff_prompt_attn(qd, kdbf, vdbf, lams, g_subln[l], lam_init)

        q_lat = jnp.transpose(_q_absorb(q, w_uk_flat), (1, 0, 2))
        q_s = q[TP:TP + TS].reshape(TS, HA, QHEAD)
        o_lat = _mla_decode(q_lat, q_s[:, :, NOPE:NOPE + ROPE_A], ckv32[TP:TP + TS].reshape(TS, 1, KVL),
                            kpe32[TP:TP + TS, :ROPE_A].reshape(TS, 1, ROPE_A), cache_mla_ckv, cache_kpe_t,
                            page_flat, l, npages)
        o_a_s = _v_up_sample(jnp.transpose(o_lat, (1, 0, 2)), w_uv_flat)

        qd_s = qd[TP:TP + TS].reshape(TS, HB, 2, HDB).transpose(0, 2, 1, 3).reshape(TS, 2 * HB, HDB)
        o_b_s = _diff_decode(qd_s, kd32[TP:TP + TS].reshape(TS, 1, 2 * HDB), vd32[TP:TP + TS].reshape(TS, 1, 2 * HDB),
                             cache_diff_k, cache_diff_v, page_flat, l, npages, lams, g_subln[l], lam_init)
        o_b_s = jnp.pad(o_b_s.astype(BF16).reshape(TS, HB * 2 * HDB), ((0, BM - TS), (0, 0)))

        mixed = _merge((o_a, o_a_s), (o_b, o_b_s), (w_branch_a, l), (w_branch_b, l), gates)
        x = _residual_mm("attn_out", mixed, (w_out, l), x, gate_m, 1024, 40)

        if l % 2 == 0:
            h2 = _ln_mod(x, scale_f, shift_f)
            a = _ffn_gate_up(h2, w_ffn_gate[l // 2], w_ffn_up[l // 2])
            x = _residual_mm("ffn_down", a, w_ffn_down[l // 2], x, gate_f, 512, 52)
        else:
            w_router_pad = jnp.pad(w_router[l // 2], ((0, 0), (0, LANES - NE)))
            h2, idx, wts = _ln_mod(x, scale_f, shift_f, w_router_pad)
            row_token, te, n_valid, pos_tok = _route_meta(idx[:, :2])
            xs = _moe_gather(h2, row_token, n_valid)
            a = _moe_gate_up(xs, w_exp_gate[l // 2], w_exp_up[l // 2], te, n_valid)
            y = _moe_down(a, w_exp_down[l // 2], te, n_valid)
            x = _moe_combine(x, y, pos_tok, wts, gate_f)

        for buf, r in zip(rows_p, (ckv32, kpe32[:, :ROPE_A], kd32, vd32)):
            buf.append(r[:TP].reshape(NB_PROMPT, SEQ, r.shape[1]))
        for buf, r in zip(rows_s, (ckv32, kpe32[:, :ROPE_A], kd32, vd32)):
            buf.append(r[TP:TP + TS].reshape(TS, 1, r.shape[1]))

    y_prompt = _final_norm(x, g_final, NBP, 0, TP).reshape(NB_PROMPT, SEQ, D)
    y_sample = _final_norm(x, g_final, 1, NBP, BM)[:TS].reshape(TS, 1, D)
    return (y_prompt, y_sample,
            jnp.stack(rows_p[0]), jnp.stack(rows_p[1]), jnp.stack(rows_p[2]), jnp.stack(rows_p[3]),
            jnp.stack(rows_s[0]), jnp.stack(rows_s[1]), jnp.stack(rows_s[2]), jnp.stack(rows_s[3]))
```

```python
import functools
import math

import jax
import jax.numpy as jnp
from jax import lax
from jax.experimental import pallas as pl
from jax.experimental.pallas import tpu as pltpu

F32 = jnp.float32
BF16 = jnp.bfloat16

D = 2048
NB_PROMPT = 4
SEQ = 2048
DEPTH = 2
DB = 128
PAGE = 128
HA = 16
QL = 512
KVL = 512
NOPE = 128
ROPE_A = 64
VA = 128
HB = 8
HDB = 128
ROT_B = 32
DFF = 5632
NE = 8
DEXP = 7168
THETA = 500000.0
EPS = 1e-6
N_MOD = 6
SCALE_A = (NOPE + ROPE_A) ** -0.5
SCALE_B = HDB ** -0.5
LOG2E = math.log2(math.e)
OFF_CKV = QL
OFF_KPE = OFF_CKV + KVL
OFF_QB = OFF_KPE + ROPE_A
OFF_KB = OFF_QB + HB * 2 * HDB
OFF_VB = OFF_KB + 2 * HDB
OFF_GA = OFF_VB + 2 * HDB
OFF_GB = OFF_GA + D

LANES = 128
QHEAD = 2 * LANES

BM = 512
TP = NB_PROMPT * SEQ
TS = DB
TPAD = TP + BM
NBP = TP // BM
NBLK = NBP + 1
BPB = SEQ // BM
BQ = 512
PAGES_PER_STEP = 64
BMO = 256
DMA_UNROLL = 8
RPAD = ((2 * (TP + TS) + NE * BMO + BMO - 1) // BMO) * BMO
NT = RPAD // BMO
MIB = 1024 * 1024


def _cp(sem, vmem_mb):
    return pltpu.CompilerParams(dimension_semantics=sem, vmem_limit_bytes=int(vmem_mb * MIB))


def _sigmoid(x):
    return 1.0 / (1.0 + jnp.exp(-x))


def _rope(x, c, s1, s2, half):
    return x * c + pltpu.roll(x, LANES - half, 1) * s1 + pltpu.roll(x, half, 1) * s2


def _sel_mod(i, p_ref, s_ref):
    return jnp.where(i < NBP, p_ref[0], s_ref[...])


def _adaln(c_all, w_mod, b_mod):
    rows = c_all.shape[0]
    bn = 1024
    n = w_mod.shape[2]

    def kern(c_ref, w_ref, b_ref, o_ref):
        c = c_ref[...]
        a = (c * _sigmoid(c)).astype(BF16)
        o_ref[...] = jnp.dot(a, w_ref[...].astype(BF16), preferred_element_type=F32) + b_ref[...]

    return pl.pallas_call(
        kern,
        grid=(DEPTH, n // bn),
        in_specs=[
            pl.BlockSpec((rows, D), lambda l, j: (0, 0)),
            pl.BlockSpec((None, D, bn), lambda l, j: (l, 0, j)),
            pl.BlockSpec((None, 1, bn), lambda l, j: (l, 0, j)),
        ],
        out_specs=pl.BlockSpec((None, rows, bn), lambda l, j: (l, 0, j)),
        out_shape=jax.ShapeDtypeStruct((DEPTH, rows, n), F32),
        compiler_params=_cp(("arbitrary", "arbitrary"), 40),
        name="adaln",
    )(c_all, w_mod, b_mod.reshape(DEPTH, 1, n))


def _mod_specs():
    return [
        pl.BlockSpec((1, 1, D), lambda i: (jnp.minimum(i // BPB, NB_PROMPT - 1), 0, 0)),
        pl.BlockSpec((BM, D), lambda i: (0, 0)),
    ]


def _ln_mod(x, scale, shift, w_router_pad=None):
    route = w_router_pad is not None

    def kern(x_ref, scp, scs, shp, shs, *rest):
        i = pl.program_id(0)
        xv = x_ref[...]
        xn = xv * lax.rsqrt(jnp.mean(xv * xv, axis=-1, keepdims=True) + EPS)
        h = xn * (1.0 + _sel_mod(i, scp, scs)) + _sel_mod(i, shp, shs)
        if not route:
            (h_ref,) = rest
            h_ref[...] = h.astype(BF16)
            return
        rw_ref, h_ref, idx_ref, wt_ref = rest
        h_ref[...] = h
        logits = jnp.dot(h, rw_ref[...], preferred_element_type=F32, precision=lax.Precision.HIGHEST)
        lane = lax.broadcasted_iota(jnp.int32, logits.shape, 1).astype(F32)
        neg = jnp.float32(-jnp.inf)
        lg = jnp.where(lane < NE, logits, neg)
        m1 = jnp.max(lg, axis=-1, keepdims=True)
        i1 = jnp.min(jnp.where(lg == m1, lane, float(LANES)), axis=-1, keepdims=True)
        lg2 = jnp.where(lane == i1, neg, lg)
        m2 = jnp.max(lg2, axis=-1, keepdims=True)
        i2 = jnp.min(jnp.where(lg2 == m2, lane, float(LANES)), axis=-1, keepdims=True)
        e = jnp.exp(m2 - m1)
        den = 1.0 + e
        idx_ref[...] = jnp.where(lane == 0, i1, jnp.where(lane == 1, i2, 0.0)).astype(jnp.int32)
        wt_ref[...] = jnp.where(lane == 0, 1.0 / den, jnp.where(lane == 1, e / den, 0.0))

    in_specs = [pl.BlockSpec((BM, D), lambda i: (i, 0))] + _mod_specs() + _mod_specs()
    args = [x, scale[0], scale[1], shift[0], shift[1]]
    out_specs = [pl.BlockSpec((BM, D), lambda i: (i, 0))]
    out_shape = [jax.ShapeDtypeStruct((TPAD, D), F32 if route else BF16)]
    if route:
        in_specs.append(pl.BlockSpec((D, LANES), lambda i: (0, 0)))
        args.append(w_router_pad)
        out_specs += [pl.BlockSpec((BM, LANES), lambda i: (i, 0))] * 2
        out_shape += [jax.ShapeDtypeStruct((TPAD, LANES), jnp.int32), jax.ShapeDtypeStruct((TPAD, LANES), F32)]
    res = pl.pallas_call(
        kern,
        grid=(NBLK,),
        in_specs=in_specs,
        out_specs=out_specs,
        out_shape=out_shape,
        compiler_params=_cp(("arbitrary",), 40),
        name="ln_mod_route" if route else "ln_mod",
    )(*args)
    return res if route else res[0]


def _final_norm(x, g, nblk, blk0, rows):
    def kern(x_ref, g_ref, o_ref):
        xv = x_ref[...]
        o_ref[...] = xv * lax.rsqrt(jnp.mean(xv * xv, axis=-1, keepdims=True) + EPS) * g_ref[...]

    return pl.pallas_call(
        kern,
        grid=(nblk,),
        in_specs=[pl.BlockSpec((BM, D), lambda i: (i + blk0, 0)), pl.BlockSpec((1, D), lambda i: (0, 0))],
        out_specs=pl.BlockSpec((BM, D), lambda i: (i, 0)),
        out_shape=jax.ShapeDtypeStruct((rows, D), F32),
        compiler_params=_cp(("arbitrary",), 32),
        name="final_norm",
    )(x, g.reshape(1, D))


def _mm(name, xs, ws, pairs, extras, outs, epilogue, *, bn, nrb=NBLK, vmem_mb=48):
    w_arrs = [w[0] if isinstance(w, tuple) else w for w in ws]
    w_layer = [w[1] if isinstance(w, tuple) else None for w in ws]
    w_kn = [a.shape[-2:] for a in w_arrs]
    n = w_kn[0][1]
    x_ops = []
    x_slots = []
    for x in xs:
        parts = x if isinstance(x, tuple) else (x,)
        x_slots.append((len(x_ops), len(parts)))
        x_ops.extend(parts)
    nx, nw, nex, no = len(x_ops), len(ws), len(extras), len(outs)

    def kern(*refs):
        x_refs = refs[:nx]
        w_refs = refs[nx:nx + nw]
        e_refs = refs[nx + nw:nx + nw + nex]
        o_refs = refs[nx + nw + nex:nx + nw + nex + no]
        wb_refs = refs[nx + nw + nex + no:]
        j = pl.program_id(0)
        i = pl.program_id(1)

        @pl.when(i == 0)
        def _():
            for w_ref, wb_ref in zip(w_refs, wb_refs):
                wb_ref[...] = w_ref[...].astype(BF16)

        assert nrb == NBLK

        @pl.when(i < NBP)
        def _():
            accs = [jnp.dot(x_refs[x_slots[a][0]][...], wb_refs[b][...], preferred_element_type=F32)
                    for a, b in pairs]
            epilogue(accs, e_refs, o_refs, i, j)

        @pl.when(i == NBP)
        def _():
            pad = jnp.zeros((BM - TS, bn), F32)
            accs = []
            for a, b in pairs:
                lo, cnt = x_slots[a]
                xs_rows = x_refs[lo + cnt - 1][0:TS, :]
                acc = jnp.dot(xs_rows, wb_refs[b][...], preferred_element_type=F32)
                accs.append(jnp.concatenate([acc, pad], axis=0))
            epilogue(accs, e_refs, o_refs, i, j)

    in_specs = []
    for x in xs:
        if isinstance(x, tuple):
            in_specs.append(pl.BlockSpec((BM, x[0].shape[1]), lambda j, i: (jnp.minimum(i, NBP - 1), 0)))
            in_specs.append(pl.BlockSpec((BM, x[1].shape[1]), lambda j, i: (0, 0)))
        else:
            in_specs.append(pl.BlockSpec((BM, x.shape[1]), lambda j, i: (i, 0)))
    w_mode = dict(pipeline_mode=pl.Buffered(1)) if n == bn else {}
    for (k_dim, _), layer in zip(w_kn, w_layer):
        if layer is None:
            in_specs.append(pl.BlockSpec((k_dim, bn), lambda j, i: (0, j), **w_mode))
        else:
            in_specs.append(pl.BlockSpec((None, k_dim, bn), lambda j, i, layer=layer: (layer, 0, j), **w_mode))
    in_specs += [pl.BlockSpec(bs, im) for _, bs, im in extras]
    res = pl.pallas_call(
        kern,
        grid=(n // bn, nrb),
        in_specs=in_specs,
        out_specs=[pl.BlockSpec(bs, im) for _, bs, im in outs],
        out_shape=[sd for sd, _, _ in outs],
        scratch_shapes=[pltpu.VMEM((k_dim, bn), BF16) for k_dim, _ in w_kn],
        compiler_params=_cp(("arbitrary", "arbitrary"), vmem_mb),
        name=name,
    )(*x_ops, *w_arrs, *[a for a, _, _ in extras])
    return res


def _row_extra(arr, width=None):
    width = arr.shape[1] if width is None else width
    return (arr, (BM, width), lambda j, i: (i, 0))


def _const_extra(arr):
    return (arr, arr.shape, lambda j, i: (0,) * arr.ndim)


def _tile_out(rows, cols, dtype, bn):
    return (jax.ShapeDtypeStruct((rows, cols), dtype), (BM, bn), lambda j, i: (i, j))


def _mod_extras(mod, bn):
    return [
        (mod[0], (1, 1, bn), lambda j, i: (jnp.minimum(i // BPB, NB_PROMPT - 1), 0, j)),
        (mod[1], (BM, bn), lambda j, i: (0, j)),
    ]


def _proj_qd(h, w_qd, tabs_b):
    bn = 1024

    def epi(accs, e, o, i, j):
        c, s1, s2 = e[0][...], e[1][...], e[2][...]
        acc = accs[0]
        for k in range(bn // LANES):
            sl = slice(k * LANES, (k + 1) * LANES)
            o[0][:, sl] = _rope(acc[:, sl], c, s1, s2, ROT_B // 2).astype(BF16)

    return _mm("proj_qd", [h], [w_qd], [(0, 0)], [_row_extra(t) for t in tabs_b],
               [_tile_out(TPAD, HB * 2 * HDB, BF16, bn)], epi, bn=bn, vmem_mb=40)[0]


def _proj_gates(h, w_g):
    bn = 1024

    def epi(accs, e, o, i, j):
        o[0][...] = accs[0]

    return _mm("proj_gates", [h], [w_g], [(0, 0)], [], [_tile_out(TPAD, 2 * D, F32, bn)], epi,
               bn=bn, vmem_mb=40)[0]


W_SMALL = QL + KVL + 2 * HDB + 2 * HDB + LANES


def _proj_small(h, w_small, g_q, g_kv, tabs_a, tabs_b):
    def rms(x, g):
        return x * lax.rsqrt(jnp.mean(x * x, axis=-1, keepdims=True) + EPS) * g

    def epi(accs, e, o, i, j):
        acc = accs[0]
        gq, gkv = e[0][...], e[1][...]
        ca, s1a, s2a = e[2][...], e[3][...], e[4][...]
        cb, s1b, s2b = e[5][...], e[6][...], e[7][...]
        o[0][...] = rms(acc[:, 0:QL], gq).astype(BF16)
        ckv = rms(acc[:, QL:QL + KVL], gkv)
        o[1][...] = ckv
        o[2][...] = ckv.astype(BF16)
        off = QL + KVL
        for k in range(2):
            sl = slice(k * LANES, (k + 1) * LANES)
            kd = _rope(acc[:, off + k * LANES:off + (k + 1) * LANES], cb, s1b, s2b, ROT_B // 2)
            o[3][:, sl] = kd
            o[4][:, sl] = kd.astype(BF16)
        off += 2 * HDB
        vd = acc[:, off:off + 2 * HDB]
        o[5][...] = vd
        o[6][...] = vd.astype(BF16)
        off += 2 * HDB
        kpe = _rope(acc[:, off:off + LANES], ca, s1a, s2a, ROPE_A // 2)
        o[7][...] = kpe
        o[8][...] = kpe.astype(BF16)

    def full(cols, dtype):
        return (jax.ShapeDtypeStruct((TPAD, cols), dtype), (BM, cols), lambda j, i: (i, 0))

    extras = [_const_extra(g_q.reshape(1, QL)), _const_extra(g_kv.reshape(1, KVL))]
    extras += [_row_extra(t) for t in tabs_a] + [_row_extra(t) for t in tabs_b]
    outs = [full(QL, BF16), full(KVL, F32), full(KVL, BF16), full(2 * HDB, F32), full(2 * HDB, BF16),
            full(2 * HDB, F32), full(2 * HDB, BF16), full(LANES, F32), full(LANES, BF16)]
    return _mm("proj_small", [h], [w_small], [(0, 0)], extras, outs, epi, bn=W_SMALL, vmem_mb=52)


def _q_up(cqn, w_uq_pad, tabs_a):
    bn = 1024

    def epi(accs, e, o, i, j):
        c, s1, s2 = e[0][...], e[1][...], e[2][...]
        acc = accs[0]
        for k in range(bn // QHEAD):
            lo = k * QHEAD
            o[0][:, lo:lo + LANES] = acc[:, lo:lo + LANES].astype(BF16)
            o[0][:, lo + LANES:lo + QHEAD] = _rope(acc[:, lo + LANES:lo + QHEAD], c, s1, s2,
                                                   ROPE_A // 2).astype(BF16)

    return _mm("q_up", [cqn], [w_uq_pad], [(0, 0)], [_row_extra(t) for t in tabs_a],
               [_tile_out(TPAD, HA * QHEAD, BF16, bn)], epi, bn=bn, vmem_mb=32)[0]


def _kv_up(ckv_bf, kpe_bf, w_uk_flat, w_uv_flat):
    bn = 512

    def epi(accs, e, o, i, j):
        kn, v = accs
        kpe = e[0][...]
        for k in range(bn // NOPE):
            o[0][:, k * QHEAD:k * QHEAD + LANES] = kn[:, k * NOPE:(k + 1) * NOPE].astype(BF16)
            o[0][:, k * QHEAD + LANES:(k + 1) * QHEAD] = kpe
        o[1][...] = v.astype(BF16)

    outs = [
        (jax.ShapeDtypeStruct((TPAD, HA * QHEAD), BF16), (BM, 2 * bn), lambda j, i: (i, j)),
        _tile_out(TPAD, HA * VA, BF16, bn),
    ]
    return _mm("kv_up", [ckv_bf], [w_uk_flat, w_uv_flat], [(0, 0), (0, 1)], [_row_extra(kpe_bf)], outs, epi,
               bn=bn, vmem_mb=32)


def _merge(o_a, o_b, w_ba, w_bb, gates):
    bn = 512
    ncb = D // bn

    def epi(accs, e, o, i, j):
        o[0][...] = (_sigmoid(e[0][...]) * accs[0] + _sigmoid(e[1][...]) * accs[1]).astype(BF16)

    extras = [(gates, (BM, bn), lambda j, i: (i, j)), (gates, (BM, bn), lambda j, i: (i, j + ncb))]
    return _mm("merge", [o_a, o_b], [w_ba, w_bb], [(0, 0), (1, 1)], extras, [_tile_out(TPAD, D, BF16, bn)], epi,
               bn=bn, vmem_mb=48)[0]


def _residual_mm(name, a, w, x_old, gate, bn, vmem_mb):
    def epi(accs, e, o, i, j):
        o[0][...] = e[0][...] + _sel_mod(i, e[1], e[2]) * accs[0]

    extras = [(x_old, (BM, bn), lambda j, i: (i, j))] + _mod_extras(gate, bn)
    return _mm(name, [a], [w], [(0, 0)], extras, [_tile_out(TPAD, D, F32, bn)], epi, bn=bn, vmem_mb=vmem_mb)[0]


def _ffn_gate_up(h, w_gate, w_up):
    bn = 512

    def epi(accs, e, o, i, j):
        g, u = accs
        o[0][...] = (g * _sigmoid(g) * u).astype(BF16)

    return _mm("ffn_gate_up", [h], [w_gate, w_up], [(0, 0), (0, 1)], [], [_tile_out(TPAD, DFF, BF16, bn)], epi,
               bn=bn, vmem_mb=40)[0]


def _causal_head(q_ref, k_ref, v_ref, qi, qcols, kcols, scale, mask):
    rows = slice(qi * BQ, (qi + 1) * BQ)
    qb = q_ref[rows, qcols]
    pieces = []
    for kj in range(qi + 1):
        kb = k_ref[kj * BQ:(kj + 1) * BQ, kcols]
        s = lax.dot_general(qb, kb, (((1,), (1,)), ((), ())), preferred_element_type=F32) * (scale * LOG2E)
        if kj == qi:
            s = jnp.where(mask, s, -jnp.inf)
        pieces.append(s)
    m = functools.reduce(jnp.maximum, [jnp.max(s, axis=-1, keepdims=True) for s in pieces])
    l = None
    acc = None
    for kj, s in enumerate(pieces):
        p = jnp.exp2(s - m)
        ps = jnp.sum(p, axis=-1, keepdims=True)
        pv = jnp.dot(p.astype(BF16), v_ref[kj * BQ:(kj + 1) * BQ, :], preferred_element_type=F32)
        l = ps if l is None else l + ps
        acc = pv if acc is None else acc + pv
    return acc / l


def _causal_mask():
    r = lax.broadcasted_iota(jnp.int32, (BQ, BQ), 0)
    c = lax.broadcasted_iota(jnp.int32, (BQ, BQ), 1)
    return c <= r


def _mla_prompt_attn(q, k, v):
    def kern(q_ref, k_ref, v_ref, o_ref):
        mask = _causal_mask()
        for qi in range(SEQ // BQ):
            o = _causal_head(q_ref, k_ref, v_ref, qi, slice(None), slice(None), SCALE_A, mask)
            o_ref[qi * BQ:(qi + 1) * BQ, :] = o.astype(BF16)

    return pl.pallas_call(
        kern,
        grid=(NB_PROMPT, HA),
        in_specs=[
            pl.BlockSpec((SEQ, QHEAD), lambda b, h: (b, h)),
            pl.BlockSpec((SEQ, QHEAD), lambda b, h: (b, h)),
            pl.BlockSpec((SEQ, VA), lambda b, h: (b, h)),
        ],
        out_specs=pl.BlockSpec((SEQ, VA), lambda b, h: (b, h)),
        out_shape=jax.ShapeDtypeStruct((TP, HA * VA), BF16),
        compiler_params=_cp(("arbitrary", "arbitrary"), 48),
        name="mla_prompt_attn",
    )(q, k, v)


def _lambda_value(lq1, lk1, lq2, lk2, lam_init):
    a = jnp.sum(lq1[...] * lk1[...], axis=-1, keepdims=True)
    b = jnp.sum(lq2[...] * lk2[...], axis=-1, keepdims=True)
    return jnp.exp(a) - jnp.exp(b) + lam_init


def _subln(diff, g, lam_init):
    y = diff * lax.rsqrt(jnp.mean(diff * diff, axis=-1, keepdims=True) + EPS)
    return y * g * (1.0 - lam_init)


def _diff_prompt_attn(qd, kd, vd, lams, g_subln, lam_init):
    def kern(q_ref, k_ref, v_ref, lq1, lk1, lq2, lk2, g_ref, o_ref):
        mask = _causal_mask()
        lam = _lambda_value(lq1, lk1, lq2, lk2, lam_init)
        for qi in range(SEQ // BQ):
            o0 = _causal_head(q_ref, k_ref, v_ref, qi, slice(0, HDB), slice(0, HDB), SCALE_B, mask)
            o1 = _causal_head(q_ref, k_ref, v_ref, qi, slice(HDB, 2 * HDB), slice(HDB, 2 * HDB), SCALE_B, mask)
            o_ref[qi * BQ:(qi + 1) * BQ, :] = _subln(o0 - lam * o1, g_ref[...], lam_init).astype(BF16)

    vec = pl.BlockSpec((1, HDB), lambda b, h: (0, 0))
    return pl.pallas_call(
        kern,
        grid=(NB_PROMPT, HB),
        in_specs=[
            pl.BlockSpec((SEQ, 2 * HDB), lambda b, h: (b, h)),
            pl.BlockSpec((SEQ, 2 * HDB), lambda b, h: (b, 0)),
            pl.BlockSpec((SEQ, 2 * HDB), lambda b, h: (b, 0)),
            vec, vec, vec, vec,
            pl.BlockSpec((1, 2 * HDB), lambda b, h: (0, 0)),
        ],
        out_specs=pl.BlockSpec((SEQ, 2 * HDB), lambda b, h: (b, h)),
        out_shape=jax.ShapeDtypeStruct((TP, HB * 2 * HDB), BF16),
        compiler_params=_cp(("arbitrary", "arbitrary"), 48),
        name="diff_prompt_attn",
    )(qd, kd, vd, *[x.reshape(1, HDB) for x in lams], g_subln.reshape(1, 2 * HDB))


def _page_spec(layer, shape, k, npages):
    return pl.BlockSpec((None, None) + shape,
                        lambda b, s, pt: (layer, pt[b * npages + s * PAGES_PER_STEP + k], 0, 0))


def _mla_decode(q_lat, q_pe, ckv_new, kpe_new, cache_ckv, cache_kpe_t, page_flat, layer, npages):
    nsteps = npages // PAGES_PER_STEP
    pp = PAGES_PER_STEP
    rows = pp * PAGE

    def kern(pt_ref, ql_ref, qp_ref, cn_ref, pn_ref, *rest):
        ckv_refs = rest[:pp]
        kpe_refs = rest[pp:2 * pp]
        o_ref = rest[2 * pp]
        kbuf, pbuf, m_ref, l_ref, acc_ref = rest[2 * pp + 1:]
        s = pl.program_id(1)

        @pl.when(s == 0)
        def _():
            cn = cn_ref[...]
            s0 = (jnp.sum(ql_ref[...].astype(F32) * cn, axis=-1, keepdims=True)
                  + jnp.sum(qp_ref[...].astype(F32) * pn_ref[...], axis=-1, keepdims=True)) * SCALE_A
            m_ref[...] = s0
            l_ref[...] = jnp.ones(l_ref.shape, F32)
            acc_ref[...] = jnp.broadcast_to(cn, acc_ref.shape)

        for k in range(pp):
            kbuf[k * PAGE:(k + 1) * PAGE, :] = ckv_refs[k][...].astype(BF16)
            pbuf[:, k * PAGE:(k + 1) * PAGE] = kpe_refs[k][...].astype(BF16)
        sc = lax.dot_general(ql_ref[...], kbuf[...], (((1,), (1,)), ((), ())), preferred_element_type=F32)
        sc = (sc + jnp.dot(qp_ref[...], pbuf[...], preferred_element_type=F32)) * SCALE_A
        m_old = m_ref[...]
        m_new = jnp.maximum(m_old, jnp.max(sc, axis=-1, keepdims=True))
        corr = jnp.exp(m_old - m_new)
        p = jnp.exp(sc - m_new)
        l_ref[...] = l_ref[...] * corr + jnp.sum(p, axis=-1, keepdims=True)
        acc_ref[...] = acc_ref[...] * corr + jnp.dot(p.astype(BF16), kbuf[...], preferred_element_type=F32)
        m_ref[...] = m_new

        @pl.when(s == nsteps - 1)
        def _():
            o_ref[...] = (acc_ref[...] / l_ref[...]).astype(BF16)

    in_specs = [
        pl.BlockSpec((None, HA, KVL), lambda b, s, pt: (b, 0, 0)),
        pl.BlockSpec((None, HA, ROPE_A), lambda b, s, pt: (b, 0, 0)),
        pl.BlockSpec((None, 1, KVL), lambda b, s, pt: (b, 0, 0)),
        pl.BlockSpec((None, 1, ROPE_A), lambda b, s, pt: (b, 0, 0)),
    ]
    in_specs += [_page_spec(layer, (PAGE, KVL), k, npages) for k in range(pp)]
    in_specs += [_page_spec(layer, (ROPE_A, PAGE), k, npages) for k in range(pp)]
    grid_spec = pltpu.PrefetchScalarGridSpec(
        num_scalar_prefetch=1,
        grid=(DB, nsteps),
        in_specs=in_specs,
        out_specs=pl.BlockSpec((None, HA, KVL), lambda b, s, pt: (b, 0, 0)),
        scratch_shapes=[pltpu.VMEM((rows, KVL), BF16), pltpu.VMEM((ROPE_A, rows), BF16),
                        pltpu.VMEM((HA, 1), F32), pltpu.VMEM((HA, 1), F32), pltpu.VMEM((HA, KVL), F32)],
    )
    return pl.pallas_call(
        kern,
        grid_spec=grid_spec,
        out_shape=jax.ShapeDtypeStruct((DB, HA, KVL), BF16),
        compiler_params=_cp(("arbitrary", "arbitrary"), 58),
        name="mla_decode",
    )(page_flat, q_lat, q_pe, ckv_new, kpe_new, *([cache_ckv] * pp), *([cache_kpe_t] * pp))


def _diff_decode(q_s, knew, vnew, cache_k, cache_v, page_flat, layer, npages, lams, g_subln, lam_init):
    nsteps = npages // PAGES_PER_STEP
    pp = PAGES_PER_STEP
    rows = pp * PAGE
    nrow = 2 * HB

    def kern(pt_ref, q_ref, kn_ref, vn_ref, lq1, lk1, lq2, lk2, g_ref, *rest):
        k_refs = rest[:pp]
        v_refs = rest[pp:2 * pp]
        o_ref = rest[2 * pp]
        kbuf, vbuf, m_ref, l_ref, acc_ref = rest[2 * pp + 1:]
        s = pl.program_id(1)
        q = q_ref[...]

        row = lax.broadcasted_iota(jnp.int32, (nrow, 1), 0)

        def scores(kmat_or_row, vpu):
            outs = []
            for mp in range(2):
                km = kmat_or_row[:, mp * HDB:(mp + 1) * HDB]
                if vpu:
                    outs.append(jnp.sum(q.astype(F32) * km, axis=-1, keepdims=True))
                else:
                    outs.append(lax.dot_general(q, km, (((1,), (1,)), ((), ())), preferred_element_type=F32))
            return jnp.where(row < HB, outs[0], outs[1]) * SCALE_B

        @pl.when(s == 0)
        def _():
            m_ref[...] = scores(kn_ref[...], True)
            l_ref[...] = jnp.ones(l_ref.shape, F32)
            acc_ref[...] = jnp.broadcast_to(vn_ref[...], acc_ref.shape)

        for k in range(pp):
            kbuf[k * PAGE:(k + 1) * PAGE, :] = k_refs[k][...].astype(BF16)
            vbuf[k * PAGE:(k + 1) * PAGE, :] = v_refs[k][...].astype(BF16)
        sc = scores(kbuf[...], False)
        m_old = m_ref[...]
        m_new = jnp.maximum(m_old, jnp.max(sc, axis=-1, keepdims=True))
        corr = jnp.exp(m_old - m_new)
        p = jnp.exp(sc - m_new)
        l_ref[...] = l_ref[...] * corr + jnp.sum(p, axis=-1, keepdims=True)
        acc_ref[...] = acc_ref[...] * corr + jnp.dot(p.astype(BF16), vbuf[...], preferred_element_type=F32)
        m_ref[...] = m_new

        @pl.when(s == nsteps - 1)
        def _():
            o = acc_ref[...] / l_ref[...]
            lam = _lambda_value(lq1, lk1, lq2, lk2, lam_init)
            o_ref[...] = _subln(o[0:HB, :] - lam * o[HB:nrow, :], g_ref[...], lam_init)

    vec = pl.BlockSpec((1, HDB), lambda b, s, pt: (0, 0))
    in_specs = [
        pl.BlockSpec((None, nrow, HDB), lambda b, s, pt: (b, 0, 0)),
        pl.BlockSpec((None, 1, 2 * HDB), lambda b, s, pt: (b, 0, 0)),
        pl.BlockSpec((None, 1, 2 * HDB), lambda b, s, pt: (b, 0, 0)),
        vec, vec, vec, vec,
        pl.BlockSpec((1, 2 * HDB), lambda b, s, pt: (0, 0)),
    ]
    in_specs += [_page_spec(layer, (PAGE, 2 * HDB), k, npages) for k in range(pp)]
    in_specs += [_page_spec(layer, (PAGE, 2 * HDB), k, npages) for k in range(pp)]
    grid_spec = pltpu.PrefetchScalarGridSpec(
        num_scalar_prefetch=1,
        grid=(DB, nsteps),
        in_specs=in_specs,
        out_specs=pl.BlockSpec((None, HB, 2 * HDB), lambda b, s, pt: (b, 0, 0)),
        scratch_shapes=[pltpu.VMEM((rows, 2 * HDB), BF16), pltpu.VMEM((rows, 2 * HDB), BF16),
                        pltpu.VMEM((nrow, 1), F32), pltpu.VMEM((nrow, 1), F32), pltpu.VMEM((nrow, 2 * HDB), F32)],
    )
    return pl.pallas_call(
        kern,
        grid_spec=grid_spec,
        out_shape=jax.ShapeDtypeStruct((DB, HB, 2 * HDB), F32),
        compiler_params=_cp(("arbitrary", "arbitrary"), 56),
        name="diff_decode",
    )(page_flat, q_s, knew, vnew, *[x.reshape(1, HDB) for x in lams], g_subln.reshape(1, 2 * HDB),
      *([cache_k] * pp), *([cache_v] * pp))


def _q_absorb(q, w_uk_flat):
    rb = TP // TS

    def kern(q_ref, w_ref, o_ref):
        o_ref[...] = lax.dot_general(q_ref[...], w_ref[...].astype(BF16), (((1,), (1,)), ((), ())),
                                     preferred_element_type=F32).astype(BF16)

    return pl.pallas_call(
        kern,
        grid=(HA,),
        in_specs=[pl.BlockSpec((TS, NOPE), lambda h: (rb, 2 * h)), pl.BlockSpec((KVL, NOPE), lambda h: (0, h))],
        out_specs=pl.BlockSpec((None, TS, KVL), lambda h: (h, 0, 0)),
        out_shape=jax.ShapeDtypeStruct((HA, TS, KVL), BF16),
        compiler_params=_cp(("arbitrary",), 16),
        name="q_absorb",
    )(q, w_uk_flat)


def _v_up_sample(o_lat_t, w_uv_flat):
    def kern(o_ref, w_ref, out_ref):
        out_ref[0:TS, :] = jnp.dot(o_ref[...], w_ref[...].astype(BF16), preferred_element_type=F32).astype(BF16)
        out_ref[TS:BM, :] = jnp.zeros((BM - TS, VA), BF16)

    return pl.pallas_call(
        kern,
        grid=(HA,),
        in_specs=[pl.BlockSpec((None, TS, KVL), lambda h: (h, 0, 0)), pl.BlockSpec((KVL, VA), lambda h: (0, h))],
        out_specs=pl.BlockSpec((BM, VA), lambda h: (0, h)),
        out_shape=jax.ShapeDtypeStruct((BM, HA * VA), BF16),
        compiler_params=_cp(("arbitrary",), 16),
        name="v_up_sample",
    )(o_lat_t, w_uv_flat)


def _route_meta(idx):
    row = jnp.arange(TPAD, dtype=jnp.int32)
    valid = row < TP + TS
    e = jnp.where(valid[:, None], idx, NE).reshape(-1)
    onehot = (e[:, None] == jnp.arange(NE, dtype=jnp.int32)[None, :]).astype(jnp.int32)
    csum = jnp.cumsum(onehot, axis=0)
    rank = jnp.sum((csum - onehot) * onehot, axis=1)
    counts = csum[-1]
    padded = ((counts + BMO - 1) // BMO) * BMO
    gend = jnp.cumsum(padded)
    gstart = gend - padded
    is_real = e < NE
    pos = jnp.where(is_real, gstart[jnp.minimum(e, NE - 1)] + rank, RPAD).astype(jnp.int32)
    token = jnp.repeat(row, 2)
    row_token = jnp.zeros((RPAD,), jnp.int32).at[pos].set(token, mode="drop")
    tile_start = jnp.arange(NT, dtype=jnp.int32) * BMO
    total = gend[-1]
    n_valid = (total // BMO).astype(jnp.int32)
    te = jnp.searchsorted(gend, tile_start, side="right").astype(jnp.int32)
    te_last = te[jnp.maximum(n_valid - 1, 0)]
    tile_valid = tile_start < total
    te = jnp.where(tile_valid, jnp.minimum(te, NE - 1), te_last)
    pos_tok = jnp.where(is_real, pos, 0)
    return row_token, te, n_valid.reshape(1), pos_tok


def _moe_gather(h32, row_token, n_valid):
    def kern(tok_ref, nv_ref, h_ref, o_ref, buf, sem):
        r = pl.program_id(0)
        nv = nv_ref[0]
        slot = r % 2

        def row_copy(src_row, tile_slot, k):
            return pltpu.make_async_copy(h_ref.at[pl.ds(src_row, 1)], buf.at[tile_slot, pl.ds(k, 1)],
                                         sem.at[tile_slot])

        def issue(tile, tile_slot):
            def start(k, c):
                row_copy(tok_ref[tile * BMO + k], tile_slot, k).start()
                return c

            lax.fori_loop(0, BMO, start, 0, unroll=DMA_UNROLL)

        @pl.when((r == 0) & (nv > 0))
        def _():
            issue(0, 0)

        @pl.when(r + 1 < nv)
        def _():
            issue(r + 1, 1 - slot)

        @pl.when(r < nv)
        def _():
            def wait(k, c):
                row_copy(0, slot, k).wait()
                return c

            lax.fori_loop(0, BMO, wait, 0, unroll=DMA_UNROLL)
            o_ref[...] = buf[slot].astype(BF16)

        @pl.when(r >= nv)
        def _():
            o_ref[...] = jnp.zeros(o_ref.shape, BF16)

    grid_spec = pltpu.PrefetchScalarGridSpec(
        num_scalar_prefetch=2,
        grid=(NT,),
        in_specs=[pl.BlockSpec(memory_space=pl.ANY)],
        out_specs=pl.BlockSpec((BMO, D), lambda r, tok, nv: (r, 0)),
        scratch_shapes=[pltpu.VMEM((2, BMO, D), F32), pltpu.SemaphoreType.DMA((2,))],
    )
    return pl.pallas_call(
        kern,
        grid_spec=grid_spec,
        out_shape=jax.ShapeDtypeStruct((RPAD, D), BF16),
        compiler_params=_cp(("arbitrary",), 16),
        name="moe_gather",
    )(row_token, n_valid, h32)


def _moe_gate_up(xs, w_gate, w_up, te, n_valid):
    bf = 1024

    def kern(te_ref, nv_ref, x_ref, wg_ref, wu_ref, o_ref, wgb, wub):
        r = pl.program_id(1)
        prev = te_ref[jnp.maximum(r - 1, 0)]

        @pl.when((r == 0) | (te_ref[r] != prev))
        def _():
            wgb[...] = wg_ref[...].astype(BF16)
            wub[...] = wu_ref[...].astype(BF16)

        @pl.when(r < nv_ref[0])
        def _():
            x = x_ref[...]
            g = jnp.dot(x, wgb[...], preferred_element_type=F32)
            u = jnp.dot(x, wub[...], preferred_element_type=F32)
            o_ref[...] = (g * _sigmoid(g) * u).astype(BF16)

        @pl.when(r >= nv_ref[0])
        def _():
            o_ref[...] = jnp.zeros(o_ref.shape, BF16)

    grid_spec = pltpu.PrefetchScalarGridSpec(
        num_scalar_prefetch=2,
        grid=(DEXP // bf, NT),
        in_specs=[
            pl.BlockSpec((BMO, D), lambda f, r, te, nv: (jnp.minimum(r, nv[0] - 1), 0)),
            pl.BlockSpec((None, D, bf), lambda f, r, te, nv: (te[r], 0, f)),
            pl.BlockSpec((None, D, bf), lambda f, r, te, nv: (te[r], 0, f)),
        ],
        out_specs=pl.BlockSpec((BMO, bf), lambda f, r, te, nv: (r, f)),
        scratch_shapes=[pltpu.VMEM((D, bf), BF16), pltpu.VMEM((D, bf), BF16)],
    )
    return pl.pallas_call(
        kern,
        grid_spec=grid_spec,
        out_shape=jax.ShapeDtypeStruct((RPAD, DEXP), BF16),
        compiler_params=_cp(("arbitrary", "arbitrary"), 52),
        name="moe_gate_up",
    )(te, n_valid, xs, w_gate, w_up)


def _moe_down(a, w_down, te, n_valid):
    bn = 512

    def kern(te_ref, nv_ref, a_ref, w_ref, o_ref, wb):
        r = pl.program_id(1)
        prev = te_ref[jnp.maximum(r - 1, 0)]

        @pl.when((r == 0) | (te_ref[r] != prev))
        def _():
            wb[...] = w_ref[...].astype(BF16)

        @pl.when(r < nv_ref[0])
        def _():
            o_ref[...] = jnp.dot(a_ref[...], wb[...], preferred_element_type=F32)

        @pl.when(r >= nv_ref[0])
        def _():
            o_ref[...] = jnp.zeros(o_ref.shape, F32)

    grid_spec = pltpu.PrefetchScalarGridSpec(
        num_scalar_prefetch=2,
        grid=(D // bn, NT),
        in_specs=[
            pl.BlockSpec((BMO, DEXP), lambda n, r, te, nv: (jnp.minimum(r, nv[0] - 1), 0)),
            pl.BlockSpec((None, DEXP, bn), lambda n, r, te, nv: (te[r], 0, n)),
        ],
        out_specs=pl.BlockSpec((BMO, bn), lambda n, r, te, nv: (r, n)),
        scratch_shapes=[pltpu.VMEM((DEXP, bn), BF16)],
    )
    return pl.pallas_call(
        kern,
        grid_spec=grid_spec,
        out_shape=jax.ShapeDtypeStruct((RPAD, D), F32),
        compiler_params=_cp(("arbitrary", "arbitrary"), 56),
        name="moe_down",
    )(te, n_valid, a, w_down)


def _moe_combine(x_old, y, pos_tok, wts, gate):
    def kern(pos_ref, x_ref, gp, gs, wt_ref, y_ref, o_ref, buf0, buf1, sem):
        i = pl.program_id(0)
        slot = i % 2

        def row_copies(p0, p1, blk_slot, k):
            return (pltpu.make_async_copy(y_ref.at[pl.ds(p0, 1)], buf0.at[blk_slot, pl.ds(k, 1)], sem.at[blk_slot]),
                    pltpu.make_async_copy(y_ref.at[pl.ds(p1, 1)], buf1.at[blk_slot, pl.ds(k, 1)], sem.at[blk_slot]))

        def issue(blk, blk_slot):
            def start(k, c):
                t = blk * BM + k
                c0, c1 = row_copies(pos_ref[2 * t], pos_ref[2 * t + 1], blk_slot, k)
                c0.start()
                c1.start()
                return c

            lax.fori_loop(0, BM, start, 0, unroll=DMA_UNROLL)

        @pl.when(i == 0)
        def _():
            issue(0, 0)

        @pl.when(i + 1 < NBLK)
        def _():
            issue(i + 1, 1 - slot)

        def wait(k, c):
            c0, c1 = row_copies(0, 0, slot, k)
            c0.wait()
            c1.wait()
            return c

        lax.fori_loop(0, BM, wait, 0, unroll=DMA_UNROLL)
        wt = wt_ref[...]
        mixed = wt[:, 0:1] * buf0[slot] + wt[:, 1:2] * buf1[slot]
        o_ref[...] = x_ref[...] + _sel_mod(i, gp, gs) * mixed

    grid_spec = pltpu.PrefetchScalarGridSpec(
        num_scalar_prefetch=1,
        grid=(NBLK,),
        in_specs=[
            pl.BlockSpec((BM, D), lambda i, pos: (i, 0)),
            pl.BlockSpec((1, 1, D), lambda i, pos: (jnp.minimum(i // BPB, NB_PROMPT - 1), 0, 0)),
            pl.BlockSpec((BM, D), lambda i, pos: (0, 0)),
            pl.BlockSpec((BM, LANES), lambda i, pos: (i, 0)),
            pl.BlockSpec(memory_space=pl.ANY),
        ],
        out_specs=pl.BlockSpec((BM, D), lambda i, pos: (i, 0)),
        scratch_shapes=[pltpu.VMEM((2, BM, D), F32), pltpu.VMEM((2, BM, D), F32), pltpu.SemaphoreType.DMA((2,))],
    )
    return pl.pallas_call(
        kern,
        grid_spec=grid_spec,
        out_shape=jax.ShapeDtypeStruct((TPAD, D), F32),
        compiler_params=_cp(("arbitrary",), 48),
        name="moe_combine",
    )(pos_tok, x_old, gate[0], gate[1], wts, y)


def _rope_tables(pos, rot_dim, period):
    half = rot_dim // 2
    lane = jnp.arange(LANES)
    within = lane % period
    active = within < rot_dim
    first = (within % rot_dim) < half
    inv_freq = THETA ** (-jnp.arange(half, dtype=F32) / half)
    ang = pos.astype(F32)[:, None] * inv_freq[None, :]
    cos = jnp.cos(ang)[:, within % half]
    sin = jnp.sin(ang)[:, within % half]
    c = jnp.where(active[None, :], cos, 1.0)
    s1 = jnp.where((active & first)[None, :], -sin, 0.0)
    s2 = jnp.where((active & ~first)[None, :], sin, 0.0)
    return c, s1, s2


def _split_mod(m):
    return m[:NB_PROMPT].reshape(NB_PROMPT, 1, D), jnp.pad(m[NB_PROMPT:NB_PROMPT + TS], ((0, BM - TS), (0, 0)))


def kernel(x_prompt, x_sample, c_prompt, c_sample, cache_mla_ckv, cache_mla_kpe, cache_diff_k, cache_diff_v, page_table, w_mod, b_mod, w_in, g_q_norm, w_uq, g_kv_norm, w_uk, w_uv, lambda_q1, lambda_k1, lambda_q2, lambda_k2, g_subln, w_branch_a, w_branch_b, w_out, w_ffn_gate, w_ffn_up, w_ffn_down, w_router, w_exp_gate, w_exp_up, w_exp_down, g_final):
    npages = page_table.shape[1]
    past_len = npages * cache_mla_ckv.shape[2]
    page_flat = page_table.reshape(-1).astype(jnp.int32)
    cache_kpe_t = jnp.swapaxes(cache_mla_kpe, 2, 3)

    x = jnp.concatenate([x_prompt.reshape(TP, D), x_sample.reshape(TS, D), jnp.zeros((BM - TS, D), F32)], axis=0)
    c_rows = NB_PROMPT + TS
    c_pad = (-c_rows) % 8
    c_all = jnp.concatenate([c_prompt, c_sample, jnp.zeros((c_pad, D), F32)], axis=0)
    mods = _adaln(c_all, w_mod, b_mod)

    pos = jnp.concatenate([jnp.tile(jnp.arange(SEQ, dtype=jnp.int32), NB_PROMPT),
                           jnp.full((TS,), past_len, jnp.int32), jnp.zeros((BM - TS,), jnp.int32)])
    tabs_a = _rope_tables(pos, ROPE_A, ROPE_A)
    tabs_b = _rope_tables(pos, ROT_B, LANES)

    rows_p = ([], [], [], [])
    rows_s = ([], [], [], [])
    for l in range(DEPTH):
        lam_init = 0.8 - 0.6 * math.exp(-0.3 * l)
        lams = (lambda_q1[l], lambda_k1[l], lambda_q2[l], lambda_k2[l])
        m6 = [_split_mod(mods[l, :, k * D:(k + 1) * D]) for k in range(N_MOD)]
        shift_m, scale_m, gate_m, shift_f, scale_f, gate_f = m6

        wl = w_in[l]
        w_qd = wl[:, OFF_QB:OFF_KB]
        w_g = wl[:, OFF_GA:]
        w_small = jnp.concatenate([wl[:, :OFF_KPE], wl[:, OFF_KB:OFF_GA], wl[:, OFF_KPE:OFF_QB],
                                   jnp.zeros((D, LANES - ROPE_A), F32)], axis=1)
        wq = w_uq[l].reshape(QL, HA, NOPE + ROPE_A)
        w_uq_pad = jnp.concatenate([wq, jnp.zeros((QL, HA, QHEAD - NOPE - ROPE_A), F32)], axis=2).reshape(QL, HA * QHEAD)
        w_uk_flat = w_uk[l].reshape(KVL, HA * NOPE)
        w_uv_flat = w_uv[l].reshape(KVL, HA * VA)

        h = _ln_mod(x, scale_m, shift_m)
        qd = _proj_qd(h, w_qd, tabs_b)
        gates = _proj_gates(h, w_g)
        cqn, ckv32, ckvbf, kd32, kdbf, vd32, vdbf, kpe32, kpebf = _proj_small(
            h, w_small, g_q_norm[l], g_kv_norm[l], tabs_a, tabs_b)
        q = _q_up(cqn, w_uq_pad, tabs_a)
        kfull, v = _kv_up(ckvbf, kpebf, w_uk_flat, w_uv_flat)

        o_a = _mla_prompt_attn(q, kfull, v)
        o_b = _diff_prompt_attn(qd, kdbf, vdbf, lams, g_subln[l], lam_init)

        q_lat = jnp.transpose(_q_absorb(q, w_uk_flat), (1, 0, 2))
        q_s = q[TP:TP + TS].reshape(TS, HA, QHEAD)
        o_lat = _mla_decode(q_lat, q_s[:, :, NOPE:NOPE + ROPE_A], ckv32[TP:TP + TS].reshape(TS, 1, KVL),
                            kpe32[TP:TP + TS, :ROPE_A].reshape(TS, 1, ROPE_A), cache_mla_ckv, cache_kpe_t,
                            page_flat, l, npages)
        o_a_s = _v_up_sample(jnp.transpose(o_lat, (1, 0, 2)), w_uv_flat)

        qd_s = qd[TP:TP + TS].reshape(TS, HB, 2, HDB).transpose(0, 2, 1, 3).reshape(TS, 2 * HB, HDB)
        o_b_s = _diff_decode(qd_s, kd32[TP:TP + TS].reshape(TS, 1, 2 * HDB), vd32[TP:TP + TS].reshape(TS, 1, 2 * HDB),
                             cache_diff_k, cache_diff_v, page_flat, l, npages, lams, g_subln[l], lam_init)
        o_b_s = jnp.pad(o_b_s.astype(BF16).reshape(TS, HB * 2 * HDB), ((0, BM - TS), (0, 0)))

        mixed = _merge((o_a, o_a_s), (o_b, o_b_s), (w_branch_a, l), (w_branch_b, l), gates)
        x = _residual_mm("attn_out", mixed, (w_out, l), x, gate_m, 1024, 40)

        if l % 2 == 0:
            h2 = _ln_mod(x, scale_f, shift_f)
            a = _ffn_gate_up(h2, w_ffn_gate[l // 2], w_ffn_up[l // 2])
            x = _residual_mm("ffn_down", a, w_ffn_down[l // 2], x, gate_f, 512, 52)
        else:
            w_router_pad = jnp.pad(w_router[l // 2], ((0, 0), (0, LANES - NE)))
            h2, idx, wts = _ln_mod(x, scale_f, shift_f, w_router_pad)
            row_token, te, n_valid, pos_tok = _route_meta(idx[:, :2])
            xs = _moe_gather(h2, row_token, n_valid)
            a = _moe_gate_up(xs, w_exp_gate[l // 2], w_exp_up[l // 2], te, n_valid)
            y = _moe_down(a, w_exp_down[l // 2], te, n_valid)
            x = _moe_combine(x, y, pos_tok, wts, gate_f)

        for buf, r in zip(rows_p, (ckv32, kpe32[:, :ROPE_A], kd32, vd32)):
            buf.append(r[:TP].reshape(NB_PROMPT, SEQ, r.shape[1]))
        for buf, r in zip(rows_s, (ckv32, kpe32[:, :ROPE_A], kd32, vd32)):
            buf.append(r[TP:TP + TS].reshape(TS, 1, r.shape[1]))

    y_prompt = _final_norm(x, g_final, NBP, 0, TP).reshape(NB_PROMPT, SEQ, D)
    y_sample = _final_norm(x, g_final, 1, NBP, BM)[:TS].reshape(TS, 1, D)
    return (y_prompt, y_sample,
            jnp.stack(rows_p[0]), jnp.stack(rows_p[1]), jnp.stack(rows_p[2]), jnp.stack(rows_p[3]),
            jnp.stack(rows_s[0]), jnp.stack(rows_s[1]), jnp.stack(rows_s[2]), jnp.stack(rows_s[3]))
```
